```python
import math
import jax, jax.numpy as jnp
from jax import lax
import numpy as np

D_MODEL = 1024
BATCH = 32
SEQ = 256
DEPTH = 2
DEC_BATCH = 4
DEC_SEQ = 4096
PAST_LEN = 512

GRID_W = 64
CHUNK = 128
Q_BLOCK = 128
H_SSD = 8
SSD_HEAD_DIM = 64
D_SSD = H_SSD * SSD_HEAD_DIM
SSD_GROUPS = 2
D_STATE = 64
CONV_W = 5
D_XBC = D_SSD + 2 * SSD_GROUPS * D_STATE
H_DA = 4
DA_HALF = 32
DA_HEAD_DIM = 2 * DA_HALF
D_DA = H_DA * DA_HEAD_DIM
H_RET = 4
RET_HEAD_DIM = 64
D_RET = H_RET * RET_HEAD_DIM
D_MIX = D_SSD + D_DA + D_RET
PROJ_SPLITS = (D_SSD, D_SSD + D_XBC, D_SSD + D_XBC + 2 * H_SSD, D_SSD + D_XBC + 2 * H_SSD + 3 * D_DA)
D_PROJ = D_SSD + D_XBC + 2 * H_SSD + 3 * D_DA + 4 * D_RET
N_EXPERTS = 16
N_EXPERT_GROUPS = 4
EXPERTS_PER_GROUP = N_EXPERTS // N_EXPERT_GROUPS
TOP_K = 2
D_FF = 512
ROPE_BASE = 10000.0
EPS = 1e-6
ALPHA = (2 * DEPTH) ** 0.25
BETA = (8 * DEPTH) ** -0.25

kernel_name = "hybrid_ssd_diffattn_retention_moe_dit_step"


def layer_norm(x, g, b):
    xf = x.astype(jnp.float32)
    mu = jnp.mean(xf, -1, keepdims=True)
    var = jnp.mean(jnp.square(xf - mu), -1, keepdims=True)
    return ((xf - mu) * lax.rsqrt(var + EPS) * g + b).astype(x.dtype)


def rms_norm(x, w):
    xf = x.astype(jnp.float32)
    return (xf * lax.rsqrt(jnp.mean(jnp.square(xf), -1, keepdims=True) + EPS) * w).astype(x.dtype)


def rope_tables(rows, dim):
    nf = dim // 4
    inv = ROPE_BASE ** (-jnp.arange(nf, dtype=jnp.float32) / nf)
    t = jnp.arange(rows * GRID_W)
    r = (t // GRID_W).astype(jnp.float32)[:, None] * inv
    c = (t % GRID_W).astype(jnp.float32)[:, None] * inv
    ang = jnp.concatenate([r, r, c, c], axis=-1)
    return jnp.cos(ang), jnp.sin(ang)


def apply_rope(x, cos, sin):
    d = x.shape[-1]
    m, nf = d // 2, d // 4
    rot = jnp.concatenate([-x[..., nf:m], x[..., :nf], -x[..., m + nf:], x[..., m:m + nf]], axis=-1)
    return (x * cos + rot * sin).astype(x.dtype)


def centred_dwconv(x, w, b):
    y = lax.conv_general_dilated(x, w[:, None, :], window_strides=(1,),
                                 padding=[(CONV_W // 2, CONV_W // 2)],
                                 dimension_numbers=('NWC', 'WIO', 'NWC'),
                                 feature_group_count=x.shape[-1])
    return y + b


def chunked_scan(q, k, v, a, h0):
    B, L, H, N = q.shape
    P = v.shape[-1]
    nc = L // CHUNK
    f32 = jnp.float32
    qc = q.astype(f32).reshape(B, nc, CHUNK, H, N)
    kc = k.astype(f32).reshape(B, nc, CHUNK, H, N)
    vc = v.astype(f32).reshape(B, nc, CHUNK, H, P)
    ah = jnp.moveaxis(jnp.cumsum(a.astype(f32).reshape(B, nc, CHUNK, H), axis=2), -1, 2)
    mask = jnp.tril(jnp.ones((CHUNK, CHUNK), dtype=bool))
    decay = jnp.exp(jnp.where(mask, ah[..., :, None] - ah[..., None, :], -jnp.inf))
    scores = jnp.einsum('bcqhn,bckhn->bchqk', qc, kc) * decay
    y_intra = jnp.einsum('bchqk,bckhp->bcqhp', scores, vc)
    chunk_state = jnp.einsum('bckhn,bchk,bckhp->bchnp', kc, jnp.exp(ah[..., -1:] - ah), vc)
    chunk_decay = jnp.exp(ah[..., -1])

    def step(h, inp):
        s, d = inp
        return d[..., None, None] * h + s, h

    h_final, h_prev = lax.scan(step, h0.astype(f32),
                               (jnp.moveaxis(chunk_state, 1, 0), jnp.moveaxis(chunk_decay, 1, 0)))
    h_prev = jnp.moveaxis(h_prev, 0, 1)
    y_inter = jnp.einsum('bcqhn,bchq,bchnp->bcqhp', qc, jnp.exp(ah), h_prev)
    return (y_intra + y_inter).reshape(B, L, H, P).astype(v.dtype), h_final


def bidir_scan(q, k_f, k_b, v, a_f, a_b, h0_f, h0_b):
    flip = lambda t: jnp.flip(t, axis=1)
    y_f, h_f = chunked_scan(q, k_f, v, a_f, h0_f)
    y_b, h_b = chunked_scan(flip(q), flip(k_b), flip(v), flip(a_b), h0_b)
    return y_f + flip(y_b), h_f, h_b


def diff_attention(q, k, v, lam):
    B, Lq = q.shape[:2]
    nb = Lq // Q_BLOCK
    qb = jnp.moveaxis(q.reshape(B, nb, Q_BLOCK, H_DA, 2, DA_HALF), 1, 0)

    def block(qi):
        s = jnp.einsum('bqhid,bkhid->bhiqk', qi, k).astype(jnp.float32) * (DA_HALF ** -0.5)
        p = jax.nn.softmax(s, axis=-1)
        att = p[:, :, 0] - lam * p[:, :, 1]
        return jnp.einsum('bhqk,bkhe->bqhe', att.astype(v.dtype), v)

    o = lax.map(block, qb)
    return jnp.moveaxis(o, 0, 1).reshape(B, Lq, H_DA, DA_HEAD_DIM)


def head_group_norm(x, w):
    xf = x.astype(jnp.float32)
    mu = jnp.mean(xf, -1, keepdims=True)
    var = jnp.mean(jnp.square(xf - mu), -1, keepdims=True)
    return ((xf - mu) * lax.rsqrt(var + EPS) * w).astype(x.dtype)


def mixer(u, P, l, ctx, rope):
    Bsz, L, _ = u.shape
    f32 = jnp.float32
    proj = u @ P['w_in'][l]
    z, xbc, dt_raw, da_p, ret_p = jnp.split(proj, PROJ_SPLITS, axis=-1)

    xbc = jax.nn.silu(centred_dwconv(xbc, P['ssd_conv_w'][l], P['ssd_conv_b'][l]))
    xs, bm, cm = jnp.split(xbc, (D_SSD, D_SSD + SSD_GROUPS * D_STATE), axis=-1)
    xs = xs.reshape(Bsz, L, H_SSD, SSD_HEAD_DIM)
    rep = H_SSD // SSD_GROUPS
    bm = jnp.repeat(bm.reshape(Bsz, L, SSD_GROUPS, D_STATE), rep, axis=2)
    cm = jnp.repeat(cm.reshape(Bsz, L, SSD_GROUPS, D_STATE), rep, axis=2)
    dt = jax.nn.softplus(dt_raw.reshape(Bsz, L, 2, H_SSD).astype(f32) + P['ssd_dt_bias'][l])
    log_a = dt * (-jnp.exp(P['ssd_a_log'][l].astype(f32)))
    h_ssd0 = jnp.zeros((Bsz, 2, H_SSD, D_STATE, SSD_HEAD_DIM), f32) if ctx is None else ctx[2]
    y_ssd, hs_f, hs_b = bidir_scan(cm, bm * dt[:, :, 0, :, None], bm * dt[:, :, 1, :, None], xs,
                                   log_a[:, :, 0], log_a[:, :, 1], h_ssd0[:, 0], h_ssd0[:, 1])
    y_ssd = y_ssd + P['ssd_d'][l][:, None] * xs
    y_ssd = rms_norm(y_ssd.reshape(Bsz, L, D_SSD) * jax.nn.silu(z), P['ssd_norm_w'][l])

    q, k, v = jnp.split(da_p, 3, axis=-1)
    q = q.reshape(Bsz, L, H_DA, 2, DA_HALF)
    k = k.reshape(Bsz, L, H_DA, 2, DA_HALF)
    v = v.reshape(Bsz, L, H_DA, DA_HEAD_DIM)
    if ctx is None:
        keys, vals = k, v
    else:
        cos_da, sin_da, cos_ret, sin_ret = rope
        cda, sda = cos_da[:, None, None, :], sin_da[:, None, None, :]
        q = apply_rope(q, cda, sda)
        keys = jnp.concatenate([apply_rope(k, cda, sda),
                                ctx[0].reshape(Bsz, -1, H_DA, 2, DA_HALF)], axis=1)
        vals = jnp.concatenate([v, ctx[1]], axis=1)
    lq1, lk1, lq2, lk2 = P['da_lambda'][l]
    lam_init = 0.8 - 0.6 * math.exp(-0.3 * l)
    lam = (jnp.exp(jnp.sum(lq1.astype(f32) * lk1)) - jnp.exp(jnp.sum(lq2.astype(f32) * lk2)) + lam_init)
    o = diff_attention(q, keys, vals, lam)
    y_da = (rms_norm(o, P['da_norm_w'][l]) * (1.0 - lam_init)).reshape(Bsz, L, D_DA)

    rq, rk, rv, rg = [t.reshape(Bsz, L, H_RET, RET_HEAD_DIM) for t in jnp.split(ret_p, 4, axis=-1)]
    if ctx is not None:
        cr, sr = cos_ret[:, None, :], sin_ret[:, None, :]
        rq = apply_rope(rq, cr, sr)
        rk = apply_rope(rk, cr, sr)
    rq = rq * (RET_HEAD_DIM ** -0.5)
    log_gamma = jax.nn.log_sigmoid(P['ret_decay'][l].astype(f32))
    a_f = jnp.broadcast_to(log_gamma[0], (Bsz, L, H_RET))
    a_b = jnp.broadcast_to(log_gamma[1], (Bsz, L, H_RET))
    h_ret0 = jnp.zeros((Bsz, 2, H_RET, RET_HEAD_DIM, RET_HEAD_DIM), f32) if ctx is None else ctx[3]
    y_ret, hr_f, hr_b = bidir_scan(rq, rk, rk, rv, a_f, a_b, h_ret0[:, 0], h_ret0[:, 1])
    y_ret = head_group_norm(y_ret, P['ret_norm_w'][l])
    y_ret = (jax.nn.silu(rg) * y_ret).reshape(Bsz, L, D_RET)

    out = jnp.concatenate([y_ssd, y_da, y_ret], axis=-1) @ P['w_out'][l]
    if ctx is None:
        new = (k.reshape(Bsz, L, H_DA, DA_HEAD_DIM), v,
               jnp.stack([hs_f, hs_b], axis=1), jnp.stack([hr_f, hr_b], axis=1))
    else:
        new = None
    return out, new


def moe(u, P, l):
    Bsz, L, D = u.shape
    t = u.reshape(-1, D)
    scores = jax.nn.softmax((t @ P['router_w']).astype(jnp.float32), axis=-1)
    sel = scores + P['router_b'].astype(jnp.float32)
    grp_score = lax.top_k(sel.reshape(-1, N_EXPERT_GROUPS, EXPERTS_PER_GROUP), 2)[0].sum(-1)
    grp = jnp.argmax(grp_score, axis=-1)
    in_grp = (jnp.arange(N_EXPERTS) // EXPERTS_PER_GROUP)[None, :] == grp[:, None]
    _, idx = lax.top_k(jnp.where(in_grp, sel, -jnp.inf), TOP_K)
    w = jnp.take_along_axis(scores, idx, axis=-1)
    w = w / jnp.sum(w, -1, keepdims=True)
    combine = jnp.sum(jax.nn.one_hot(idx, N_EXPERTS, dtype=jnp.float32) * w[..., None], axis=1).astype(t.dtype)
    out = jnp.zeros_like(t)
    for e in range(N_EXPERTS):
        h = jax.nn.silu(t @ P['moe_w_gate'][l, e]) * (t @ P['moe_w_up'][l, e])
        out = out + combine[:, e:e + 1] * (h @ P['moe_w_down'][l, e])
    return out.reshape(Bsz, L, D)


def layer(x, cond, P, l, ctx, rope):
    mod = (jax.nn.silu(cond) @ P['w_ada'][l] + P['b_ada'][l])[:, None, :]
    sh1, sc1, g1, sh2, sc2, g2 = jnp.split(mod, 6, axis=-1)
    mix, new = mixer(x * (1 + sc1) + sh1, P, l, ctx, rope)
    x = layer_norm(ALPHA * x + g1 * mix, P['ln_mix_g'][l], P['ln_mix_b'][l])
    x = layer_norm(ALPHA * x + g2 * moe(x * (1 + sc2) + sh2, P, l), P['ln_ffn_g'][l], P['ln_ffn_b'][l])
    return x, new


def setup_inputs(seed: int = 0) -> dict:
    key = jax.random.key(seed)
    ks = jax.random.split(key, 32)
    f32 = jnp.float32
    nrm = lambda k, s, sc: jax.random.normal(k, s, f32) * sc
    dt0 = jnp.exp(jax.random.uniform(ks[11], (DEPTH, 2, H_SSD), f32, math.log(1e-3), math.log(1e-1)))
    gamma = 1.0 - 2.0 ** (-5.0 - jnp.arange(H_RET, dtype=f32))
    ret_raw = jnp.log(gamma) - jnp.log1p(-gamma)
    return {
        'x_prompt': nrm(ks[0], (BATCH, SEQ, D_MODEL), 1.0),
        'x_sample': nrm(ks[1], (DEC_BATCH, DEC_SEQ, D_MODEL), 1.0),
        'c': nrm(ks[2], (DEC_BATCH, D_MODEL), 1.0),
        'cache_da_k': nrm(ks[3], (DEC_BATCH, DEPTH, PAST_LEN, H_DA, DA_HEAD_DIM), 1.0),
        'cache_da_v': nrm(ks[4], (DEC_BATCH, DEPTH, PAST_LEN, H_DA, DA_HEAD_DIM), 1.0),
        'state_ssd': nrm(ks[5], (DEC_BATCH, DEPTH, 2, H_SSD, D_STATE, SSD_HEAD_DIM), 0.5),
        'state_ret': nrm(ks[6], (DEC_BATCH, DEPTH, 2, H_RET, RET_HEAD_DIM, RET_HEAD_DIM), 0.5),
        'c_ctx': nrm(ks[7], (D_MODEL,), 1.0),
        'w_ada': nrm(ks[8], (DEPTH, D_MODEL, 6 * D_MODEL), 0.5 * D_MODEL ** -0.5),
        'b_ada': nrm(ks[9], (DEPTH, 6 * D_MODEL), 0.01),
        'w_in': nrm(ks[10], (DEPTH, D_MODEL, D_PROJ), D_MODEL ** -0.5),
        'ssd_conv_w': nrm(ks[12], (DEPTH, CONV_W, D_XBC), CONV_W ** -0.5),
        'ssd_conv_b': nrm(ks[13], (DEPTH, D_XBC), 0.01),
        'ssd_dt_bias': dt0 + jnp.log(-jnp.expm1(-dt0)),
        'ssd_a_log': jnp.log(jax.random.uniform(ks[14], (DEPTH, 2, H_SSD), f32, 1.0, 16.0)),
        'ssd_d': 1.0 + nrm(ks[15], (DEPTH, H_SSD), 0.01),
        'ssd_norm_w': 1.0 + nrm(ks[16], (DEPTH, D_SSD), 0.01),
        'da_lambda': nrm(ks[17], (DEPTH, 4, DA_HALF), 0.1),
        'da_norm_w': 1.0 + nrm(ks[18], (DEPTH, DA_HEAD_DIM), 0.01),
        'ret_decay': jnp.broadcast_to(ret_raw, (DEPTH, 2, H_RET)) + nrm(ks[19], (DEPTH, 2, H_RET), 0.01),
        'ret_norm_w': 1.0 + nrm(ks[20], (DEPTH, RET_HEAD_DIM), 0.01),
        'w_out': nrm(ks[21], (DEPTH, D_MIX, D_MODEL), BETA * D_MIX ** -0.5),
        'ln_mix_g': 1.0 + nrm(ks[22], (DEPTH, D_MODEL), 0.01),
        'ln_mix_b': nrm(ks[23], (DEPTH, D_MODEL), 0.01),
        'router_w': nrm(ks[24], (D_MODEL, N_EXPERTS), D_MODEL ** -0.5),
        'router_b': nrm(ks[25], (N_EXPERTS,), 0.01),
        'moe_w_gate': nrm(ks[26], (DEPTH, N_EXPERTS, D_MODEL, D_FF), D_MODEL ** -0.5),
        'moe_w_up': nrm(ks[27], (DEPTH, N_EXPERTS, D_MODEL, D_FF), D_MODEL ** -0.5),
        'moe_w_down': nrm(ks[28], (DEPTH, N_EXPERTS, D_FF, D_MODEL), BETA * D_FF ** -0.5),
        'ln_ffn_g': 1.0 + nrm(ks[29], (DEPTH, D_MODEL), 0.01),
        'ln_ffn_b': nrm(ks[30], (DEPTH, D_MODEL), 0.01),
    }


def reference(x_prompt, x_sample, c, cache_da_k, cache_da_v, state_ssd, state_ret, c_ctx,
              w_ada, b_ada, w_in, ssd_conv_w, ssd_conv_b, ssd_dt_bias, ssd_a_log, ssd_d, ssd_norm_w,
              da_lambda, da_norm_w, ret_decay, ret_norm_w, w_out, ln_mix_g, ln_mix_b,
              router_w, router_b, moe_w_gate, moe_w_up, moe_w_down, ln_ffn_g, ln_ffn_b):
    P = dict(w_ada=w_ada, b_ada=b_ada, w_in=w_in, ssd_conv_w=ssd_conv_w, ssd_conv_b=ssd_conv_b,
             ssd_dt_bias=ssd_dt_bias, ssd_a_log=ssd_a_log, ssd_d=ssd_d, ssd_norm_w=ssd_norm_w,
             da_lambda=da_lambda, da_norm_w=da_norm_w, ret_decay=ret_decay, ret_norm_w=ret_norm_w,
             w_out=w_out, ln_mix_g=ln_mix_g, ln_mix_b=ln_mix_b, router_w=router_w, router_b=router_b,
             moe_w_gate=moe_w_gate, moe_w_up=moe_w_up, moe_w_down=moe_w_down,
             ln_ffn_g=ln_ffn_g, ln_ffn_b=ln_ffn_b)

    y = x_prompt
    ks_, vs_, hs_, hr_ = [], [], [], []
    for l in range(DEPTH):
        y, (k_l, v_l, hs_l, hr_l) = layer(y, c_ctx[None, :], P, l, None, None)
        ks_.append(k_l)
        vs_.append(v_l)
        hs_.append(hs_l)
        hr_.append(hr_l)
    new_da_k = jnp.stack(ks_, axis=1)
    new_da_v = jnp.stack(vs_, axis=1)
    new_ssd = jnp.stack(hs_, axis=1)
    new_ret = jnp.stack(hr_, axis=1)

    rows = x_sample.shape[1] // GRID_W
    cos_da, sin_da = rope_tables(rows, DA_HALF)
    cos_ret, sin_ret = rope_tables(rows, RET_HEAD_DIM)
    rope = (cos_da, sin_da, cos_ret, sin_ret)
    z = x_sample
    for l in range(DEPTH):
        ctx = (cache_da_k[:, l], cache_da_v[:, l], state_ssd[:, l], state_ret[:, l])
        z, _ = layer(z, c, P, l, ctx, rope)

    return (y, z, new_da_k, new_da_v, new_ssd, new_ret)
```

```python
import functools
import math

import jax
import jax.numpy as jnp
from jax import lax
from jax.experimental import pallas as pl
from jax.experimental.pallas import tpu as pltpu

D_MODEL = 1024
DEPTH = 2
GRID_W = 64
CHUNK = 128
H_SSD = 8
SSD_HEAD_DIM = 64
D_SSD = H_SSD * SSD_HEAD_DIM
SSD_GROUPS = 2
D_STATE = 64
CONV_W = 5
D_XBC = D_SSD + 2 * SSD_GROUPS * D_STATE
H_DA = 4
DA_HALF = 32
DA_HEAD_DIM = 2 * DA_HALF
D_DA = H_DA * DA_HEAD_DIM
H_RET = 4
RET_HEAD_DIM = 64
D_RET = H_RET * RET_HEAD_DIM
N_EXPERTS = 16
EXPERTS_PER_GROUP = 4
D_FF = 512
ROPE_BASE = 10000.0
EPS = 1e-6
ALPHA = (2 * DEPTH) ** 0.25

F32 = jnp.float32
BF16 = jnp.bfloat16
HIGHEST = lax.Precision.HIGHEST
NEG_INF = float("-inf")

V7X_VMEM_LIMIT_BYTES = 56 * 1024 * 1024
CONV_HALO = 8

C_Z, C_XBC, C_Q, C_K, C_V, C_RQ, C_RV, C_RG, C_DT, C_QROT, C_RQROT, C_END = (
    0, 512, 1280, 1536, 1792, 2048, 2304, 2560, 2816, 2944, 3200, 3456)
R_K, R_RK, R_DT, R_KROT, R_RKROT, R_END = 0, 256, 512, 544, 800, 1056


def _silu(x):
    return x * (1.0 / (1.0 + jnp.exp(-x)))


def _softplus(x):
    return jnp.maximum(x, 0.0) + jnp.log1p(jnp.exp(-jnp.abs(x)))


def _dot(a, b, precision=None):
    return jnp.dot(a, b, preferred_element_type=F32, precision=precision)


def _layer_norm(t, g, b):
    mu = jnp.mean(t, axis=-1, keepdims=True)
    tc = t - mu
    var = jnp.mean(tc * tc, axis=-1, keepdims=True)
    return tc * lax.rsqrt(var + EPS) * g + b


def _params(sem):
    return pltpu.CompilerParams(dimension_semantics=sem, vmem_limit_bytes=V7X_VMEM_LIMIT_BYTES)


def _ada_kernel(cond_ref, w_ref, b_ref, o_ref):
    o_ref[0] = _dot(_silu(cond_ref[...]), w_ref[0], HIGHEST) + b_ref[0]


def _ada_call(cond8, w_ada, b_ada):
    tn = 1536
    nb = (6 * D_MODEL) // tn
    return pl.pallas_call(
        _ada_kernel,
        grid=(DEPTH, nb),
        in_specs=[pl.BlockSpec((8, D_MODEL), lambda l, n: (0, 0)),
                  pl.BlockSpec((1, D_MODEL, tn), lambda l, n: (l, 0, n)),
                  pl.BlockSpec((1, 1, tn), lambda l, n: (l, 0, n))],
        out_specs=pl.BlockSpec((1, 8, tn), lambda l, n: (l, 0, n)),
        out_shape=jax.ShapeDtypeStruct((DEPTH, 8, 6 * D_MODEL), F32),
        compiler_params=_params(("arbitrary", "arbitrary")),
        name="ada_mod",
    )(cond8, w_ada, b_ada.reshape(DEPTH, 1, 6 * D_MODEL))


def _in_proj_kernel(rope, *refs):
    if rope:
        (x_ref, mod_ref, wnn_ref, wnt_ref, cd_ref, sd_ref, cr_ref, sr_ref,
         cdt_ref, sdt_ref, crt_ref, srt_ref,
         z_ref, xbc_ref, dt_ref, dtt_ref, q_ref, kt_ref, v_ref, rq_ref, rkt_ref, rv_ref, rg_ref) = refs
    else:
        (x_ref, mod_ref, wnn_ref, wnt_ref,
         z_ref, xbc_ref, dt_ref, dtt_ref, q_ref, kt_ref, v_ref, rq_ref, rkt_ref, rv_ref, rg_ref,
         k_ref) = refs
    mod = mod_ref[0]
    xm = (x_ref[...] * (1.0 + mod[1:2]) + mod[0:1]).astype(BF16)

    def nn(a, b):
        return _dot(xm, wnn_ref[:, a:b])

    def nt(a, b):
        return lax.dot_general(wnt_ref[a:b, :], xm, (((1,), (1,)), ((), ())),
                               preferred_element_type=F32)

    z_ref[...] = nn(C_Z, C_XBC)
    xbc_ref[...] = nn(C_XBC, C_Q)
    dt_ref[...] = nn(C_DT, C_QROT)[:, :2 * H_SSD]
    dtt_ref[...] = nt(R_DT, R_KROT)[:2 * H_SSD, :]
    v_ref[...] = nn(C_V, C_RQ)
    rv_ref[...] = nn(C_RV, C_RG)
    rg_ref[...] = nn(C_RG, C_DT)
    q = nn(C_Q, C_K)
    rq = nn(C_RQ, C_RV)
    kt = nt(R_K, R_RK)
    rkt = nt(R_RK, R_DT)
    if rope:
        q = q * cd_ref[...] + nn(C_QROT, C_RQROT) * sd_ref[...]
        rq = rq * cr_ref[...] + nn(C_RQROT, C_END) * sr_ref[...]
        kt = kt * cdt_ref[...] + nt(R_KROT, R_RKROT) * sdt_ref[...]
        rkt = rkt * crt_ref[...] + nt(R_RKROT, R_END) * srt_ref[...]
    else:
        k_ref[...] = nn(C_K, C_V)
    q_ref[...] = (q * (DA_HALF ** -0.5)).astype(BF16)
    rq_ref[...] = (rq * (RET_HEAD_DIM ** -0.5)).astype(BF16)
    kt_ref[...] = kt.astype(BF16)
    rkt_ref[...] = rkt.astype(BF16)


def _in_proj_call(x2d, mod, wnn, wnt, seq_len, rope_tabs):
    n = x2d.shape[0]
    tm = 256
    nblk = n // tm
    per_seq = seq_len // tm
    n_mod = mod.shape[0]
    rope = rope_tabs is not None
    mod_idx = (lambda i: (i // per_seq, 0, 0)) if n_mod > 1 else (lambda i: (0, 0, 0))
    row = lambda w: pl.BlockSpec((tm, w), lambda i: (i, 0))
    col = lambda h: pl.BlockSpec((h, tm), lambda i: (0, i))
    in_specs = [row(D_MODEL),
                pl.BlockSpec((1, 8, D_MODEL), mod_idx),
                pl.BlockSpec(wnn.shape, lambda i: (0, 0)),
                pl.BlockSpec(wnt.shape, lambda i: (0, 0))]
    args = [x2d, mod, wnn, wnt]
    if rope:
        in_specs += [pl.BlockSpec((tm, 256), lambda i: (i % per_seq, 0))] * 4
        in_specs += [pl.BlockSpec((256, tm), lambda i: (0, i % per_seq))] * 4
        args += list(rope_tabs)
    sds = jax.ShapeDtypeStruct
    out_shape = [sds((n, D_SSD), F32), sds((n, D_XBC), F32), sds((n, 2 * H_SSD), F32),
                 sds((2 * H_SSD, n), F32), sds((n, D_DA), BF16), sds((D_DA, n), BF16),
                 sds((n, D_DA), F32), sds((n, D_RET), BF16), sds((D_RET, n), BF16),
                 sds((n, D_RET), F32), sds((n, D_RET), F32)]
    out_specs = [row(D_SSD), row(D_XBC), row(2 * H_SSD), col(2 * H_SSD), row(D_DA), col(D_DA),
                 row(D_DA), row(D_RET), col(D_RET), row(D_RET), row(D_RET)]
    if not rope:
        out_shape.append(sds((n, D_DA), F32))
        out_specs.append(row(D_DA))
    return pl.pallas_call(
        functools.partial(_in_proj_kernel, rope),
        grid=(nblk,),
        in_specs=in_specs,
        out_specs=out_specs,
        out_shape=out_shape,
        compiler_params=_params(("arbitrary",)),
        name="in_proj_rope" if rope else "in_proj",
    )(*args)


def _ssd_kernel(nc, has_h0, *refs):
    refs = list(refs)
    (xm_ref, xp_ref, xn_ref, z_ref, dt_ref, dtt_ref, cw_ref, cb_ref, dtb_c_ref, dtb_r_ref,
     alog_c_ref, alog_r_ref, dskip_ref, nw_ref) = refs[:14]
    refs = refs[14:]
    h0_ref = refs.pop(0) if has_h0 else None
    y_ref = refs.pop(0)
    hs_ref = None if has_h0 else refs.pop(0)
    ext_ref, hp_ref, yf_ref = refs

    d = pl.program_id(1)
    j = pl.program_id(2)
    fwd = d == 0
    cj = jnp.where(fwd, j, nc - 1 - j)

    ext_ref[0:CONV_HALO, :] = jnp.where(cj > 0, xp_ref[0], 0.0)
    ext_ref[CONV_HALO:CONV_HALO + CHUNK, :] = xm_ref[0]
    ext_ref[CONV_HALO + CHUNK:, :] = jnp.where(cj < nc - 1, xn_ref[0], 0.0)
    acc = jnp.zeros((CHUNK, D_XBC), F32) + cb_ref[...]
    for k in range(CONV_W):
        start = CONV_HALO - CONV_W // 2 + k
        acc = acc + cw_ref[k:k + 1, :] * ext_ref[start:start + CHUNK, :]
    act = _silu(acc)
    xs = act[:, :D_SSD]
    bm = act[:, D_SSD:D_SSD + 128]
    cm = act[:, D_SSD + 128:]

    dt_c_all = _softplus(dt_ref[0] + dtb_c_ref[...])
    dt_r_all = _softplus(dtt_ref[...] + dtb_r_ref[...])
    a_c_all = -jnp.exp(alog_c_ref[...])
    a_r_all = -jnp.exp(alog_r_ref[...])
    dt_c = jnp.where(fwd, dt_c_all[:, :H_SSD], dt_c_all[:, H_SSD:])
    dt_r = jnp.where(fwd, dt_r_all[:H_SSD], dt_r_all[H_SSD:])
    la_c = dt_c * jnp.where(fwd, a_c_all[:, :H_SSD], a_c_all[:, H_SSD:])
    la_r = dt_r * jnp.where(fwd, a_r_all[:H_SSD], a_r_all[H_SSD:])

    ii = lax.broadcasted_iota(jnp.int32, (CHUNK, CHUNK), 0)
    jj = lax.broadcasted_iota(jnp.int32, (CHUNK, CHUNK), 1)
    valid = jnp.where(fwd, ii - jj, jj - ii) >= 0
    valid_t = jnp.where(fwd, jj - ii, ii - jj) >= 0
    cum_c = _dot(valid.astype(F32), la_c, HIGHEST)
    cum_r = _dot(la_r, valid_t.astype(F32), HIGHEST)
    tot_r = jnp.where(fwd, cum_r[:, CHUNK - 1:], cum_r[:, :1])
    e_tot = jnp.exp(tot_r)
    wk_r = dt_r * jnp.exp(tot_r - cum_r)
    e_cum_c = jnp.exp(cum_c)

    bm_t = bm.T
    bm_t16 = bm_t.astype(BF16)
    lane = lax.broadcasted_iota(jnp.int32, (1, 128), 1)
    low_lanes = lane < 64
    feat_row = lax.broadcasted_iota(jnp.int32, (128, 1), 0)

    @pl.when(j == 0)
    def _():
        for p in range(H_SSD // 2):
            if has_h0:
                g = p // 2
                both = jnp.concatenate([h0_ref[0, pl.ds(d, 1), 2 * p][0],
                                        h0_ref[0, pl.ds(d, 1), 2 * p + 1][0]], axis=1)
                zero = jnp.zeros((D_STATE, 128), F32)
                hp_ref[p] = jnp.concatenate([both, zero] if g == 0 else [zero, both], axis=0)
            else:
                hp_ref[p] = jnp.zeros((128, 128), F32)

    ys = []
    for g in range(SSD_GROUPS):
        in_group_lane = (lane >= 64 * g) & (lane < 64 * (g + 1))
        cg = jnp.where(in_group_lane, cm, 0.0)
        gram = _dot(cg.astype(BF16), bm_t16)
        in_group_row = (feat_row >= 64 * g) & (feat_row < 64 * (g + 1))
        for p in range(2 * g, 2 * g + 2):
            xs_p = xs[:, 128 * p:128 * (p + 1)].astype(BF16)
            h_prev = hp_ref[p]
            h_prev16 = h_prev.astype(BF16)
            y_h, s_h = [], []
            for hh in (2 * p, 2 * p + 1):
                decay = jnp.exp(jnp.where(valid, cum_c[:, hh:hh + 1] - cum_r[hh:hh + 1, :], NEG_INF))
                sc = gram * decay * dt_r[hh:hh + 1, :]
                y_intra = _dot(sc.astype(BF16), xs_p)
                y_inter = _dot((cg * e_cum_c[:, hh:hh + 1]).astype(BF16), h_prev16)
                y_h.append(y_intra + y_inter)
                s_h.append(_dot((bm_t * wk_r[hh:hh + 1, :]).astype(BF16), xs_p))
            ys.append(jnp.where(low_lanes, y_h[0], y_h[1]))
            s_pair = jnp.where(low_lanes, s_h[0], s_h[1])
            e_pair = jnp.where(low_lanes, e_tot[2 * p:2 * p + 1, :], e_tot[2 * p + 1:2 * p + 2, :])
            hp_ref[p] = jnp.where(in_group_row, e_pair * h_prev + s_pair, 0.0)
    y_dir = jnp.concatenate(ys, axis=1)

    if hs_ref is not None:
        @pl.when(j == nc - 1)
        def _():
            for dd in range(2):
                @pl.when(d == dd)
                def _():
                    for p in range(H_SSD // 2):
                        g = p // 2
                        hn = hp_ref[p]
                        hs_ref[0, dd, 2 * p] = hn[64 * g:64 * (g + 1), :64]
                        hs_ref[0, dd, 2 * p + 1] = hn[64 * g:64 * (g + 1), 64:]

    @pl.when(fwd)
    def _():
        yf_ref[cj] = y_dir

    @pl.when(d == 1)
    def _():
        yt = yf_ref[cj] + y_dir + dskip_ref[...] * xs
        gated = yt * _silu(z_ref[0])
        ms = jnp.mean(gated * gated, axis=-1, keepdims=True)
        y_ref[0] = gated * lax.rsqrt(ms + EPS) * nw_ref[...]


def _ssd_call(xbc, z, dt, dtt, conv_w, conv_b, dt_bias, a_log, d_skip, norm_w, h0):
    bsz, seq_len, _ = xbc.shape
    nc = seq_len // CHUNK
    has_h0 = h0 is not None
    hb = CHUNK // CONV_HALO
    last_hb = seq_len // CONV_HALO - 1

    def chunk_of(d, j):
        return jnp.where(d == 0, j, nc - 1 - j)

    def full(shape):
        return pl.BlockSpec(shape, lambda b, d, j: (0,) * len(shape))

    in_specs = [
        pl.BlockSpec((1, CHUNK, D_XBC), lambda b, d, j: (b, chunk_of(d, j), 0)),
        pl.BlockSpec((1, CONV_HALO, D_XBC), lambda b, d, j: (b, jnp.maximum(chunk_of(d, j) * hb - 1, 0), 0)),
        pl.BlockSpec((1, CONV_HALO, D_XBC),
                     lambda b, d, j: (b, jnp.minimum((chunk_of(d, j) + 1) * hb, last_hb), 0)),
        pl.BlockSpec((1, CHUNK, D_SSD), lambda b, d, j: (b, chunk_of(d, j), 0)),
        pl.BlockSpec((1, CHUNK, 2 * H_SSD), lambda b, d, j: (b, chunk_of(d, j), 0)),
        pl.BlockSpec((2 * H_SSD, CHUNK), lambda b, d, j: (0, b * nc + chunk_of(d, j))),
        full((CONV_W, D_XBC)), full((1, D_XBC)), full((1, 2 * H_SSD)), full((2 * H_SSD, 1)),
        full((1, 2 * H_SSD)), full((2 * H_SSD, 1)), full((1, D_SSD)), full((1, D_SSD)),
    ]
    args = [xbc, xbc, xbc, z, dt, dtt, conv_w, conv_b.reshape(1, D_XBC),
            dt_bias.reshape(1, 2 * H_SSD), dt_bias.reshape(2 * H_SSD, 1),
            a_log.reshape(1, 2 * H_SSD), a_log.reshape(2 * H_SSD, 1),
            jnp.repeat(d_skip, SSD_HEAD_DIM).reshape(1, D_SSD), norm_w.reshape(1, D_SSD)]
    y_spec = pl.BlockSpec((1, CHUNK, D_SSD), lambda b, d, j: (b, jnp.where(d == 0, nc - 1, nc - 1 - j), 0))
    y_shape = jax.ShapeDtypeStruct((bsz, seq_len, D_SSD), F32)
    state_block = (1, 2, H_SSD, D_STATE, SSD_HEAD_DIM)
    state_spec = pl.BlockSpec(state_block, lambda b, d, j: (b, 0, 0, 0, 0))
    if has_h0:
        in_specs.append(state_spec)
        args.append(h0)
        out_specs, out_shape = y_spec, y_shape
    else:
        out_specs = [y_spec, state_spec]
        out_shape = [y_shape, jax.ShapeDtypeStruct((bsz,) + state_block[1:], F32)]
    return pl.pallas_call(
        functools.partial(_ssd_kernel, nc, has_h0),
        grid=(bsz, 2, nc),
        in_specs=in_specs,
        out_specs=out_specs,
        out_shape=out_shape,
        scratch_shapes=[pltpu.VMEM((CHUNK + 2 * CONV_HALO, D_XBC), F32),
                        pltpu.VMEM((H_SSD // 2, 128, 128), F32),
                        pltpu.VMEM((nc, CHUNK, D_SSD), F32)],
        compiler_params=_params(("arbitrary", "arbitrary", "arbitrary")),
        name="ssd_latent" if has_h0 else "ssd_ctx",
    )(*args)


def _ret_kernel(nc, has_h0, *refs):
    refs = list(refs)
    q_ref, kt_ref, v_ref, g_ref, dec_ref, nw_ref = refs[:6]
    refs = refs[6:]
    h0_ref = refs.pop(0) if has_h0 else None
    y_ref = refs.pop(0)
    hs_ref = None if has_h0 else refs.pop(0)
    hr_ref, yf_ref = refs

    d = pl.program_id(1)
    j = pl.program_id(2)
    fwd = d == 0
    cj = jnp.where(fwd, j, nc - 1 - j)

    dec = dec_ref[...]
    lg_all = -_softplus(-dec)
    lg = jnp.where(fwd, lg_all[0:1], lg_all[1:2])

    ii = lax.broadcasted_iota(jnp.int32, (CHUNK, CHUNK), 0)
    jj = lax.broadcasted_iota(jnp.int32, (CHUNK, CHUNK), 1)
    dist = jnp.where(fwd, ii - jj, jj - ii)
    valid = dist >= 0
    dist_f = dist.astype(F32)
    qpos = lax.broadcasted_iota(jnp.int32, (CHUNK, 1), 0)
    kpos = lax.broadcasted_iota(jnp.int32, (1, CHUNK), 1)
    n_q = jnp.where(fwd, qpos + 1, CHUNK - qpos).astype(F32)
    n_k = jnp.where(fwd, CHUNK - 1 - kpos, kpos).astype(F32)
    lane = lax.broadcasted_iota(jnp.int32, (1, 128), 1)
    low_lanes = lane < 64
    low_rows = lax.broadcasted_iota(jnp.int32, (128, 1), 0) < 64
    block_diag = low_rows == low_lanes

    @pl.when(j == 0)
    def _():
        for p in range(H_RET // 2):
            if has_h0:
                zero = jnp.zeros((RET_HEAD_DIM, RET_HEAD_DIM), F32)
                top = jnp.concatenate([h0_ref[0, pl.ds(d, 1), 2 * p][0], zero], axis=1)
                bot = jnp.concatenate([zero, h0_ref[0, pl.ds(d, 1), 2 * p + 1][0]], axis=1)
                hr_ref[p] = jnp.concatenate([top, bot], axis=0)
            else:
                hr_ref[p] = jnp.zeros((128, 128), F32)

    ys = []
    for p in range(H_RET // 2):
        q_p = q_ref[0, :, 128 * p:128 * (p + 1)].astype(F32)
        kt_p = kt_ref[128 * p:128 * (p + 1), :]
        v_p = v_ref[0, :, 128 * p:128 * (p + 1)].astype(BF16)
        h_prev = hr_ref[p]
        lg_a = lg[:, 2 * p:2 * p + 1]
        lg_b = lg[:, 2 * p + 1:2 * p + 2]
        y_h = []
        for lg_h, in_head in ((lg_a, low_lanes), (lg_b, jnp.logical_not(low_lanes))):
            s = _dot(jnp.where(in_head, q_p, 0.0).astype(BF16), kt_p)
            decay = jnp.exp(jnp.where(valid, dist_f * lg_h, NEG_INF))
            y_h.append(_dot((s * decay).astype(BF16), v_p))
        e_q = jnp.where(low_lanes, jnp.exp(n_q * lg_a), jnp.exp(n_q * lg_b))
        y_inter = _dot((q_p * e_q).astype(BF16), h_prev.astype(BF16))
        w_k = jnp.where(low_rows, jnp.exp(n_k * lg_a), jnp.exp(n_k * lg_b))
        s_new = _dot((kt_p.astype(F32) * w_k).astype(BF16), v_p)
        e_tot = jnp.where(low_lanes, jnp.exp(CHUNK * lg_a), jnp.exp(CHUNK * lg_b))
        hr_ref[p] = e_tot * h_prev + jnp.where(block_diag, s_new, 0.0)
        ys.append(jnp.where(low_lanes, y_h[0], y_h[1]) + y_inter)
    y_dir = jnp.concatenate(ys, axis=1)

    if hs_ref is not None:
        @pl.when(j == nc - 1)
        def _():
            for dd in range(2):
                @pl.when(d == dd)
                def _():
                    for p in range(H_RET // 2):
                        hn = hr_ref[p]
                        hs_ref[0, dd, 2 * p] = hn[:64, :64]
                        hs_ref[0, dd, 2 * p + 1] = hn[64:, 64:]

    @pl.when(fwd)
    def _():
        yf_ref[cj] = y_dir

    @pl.when(d == 1)
    def _():
        yt = yf_ref[cj] + y_dir
        lane4 = lax.shift_right_logical(lax.broadcasted_iota(jnp.int32, (1, D_RET), 1), 6)
        mu = jnp.zeros_like(yt)
        for h in range(H_RET):
            m_h = jnp.sum(jnp.where(lane4 == h, yt, 0.0), axis=-1, keepdims=True) * (1.0 / RET_HEAD_DIM)
            mu = jnp.where(lane4 == h, m_h, mu)
        yc = yt - mu
        sq = yc * yc
        rs = jnp.zeros_like(yt)
        for h in range(H_RET):
            v_h = jnp.sum(jnp.where(lane4 == h, sq, 0.0), axis=-1, keepdims=True) * (1.0 / RET_HEAD_DIM)
            rs = jnp.where(lane4 == h, lax.rsqrt(v_h + EPS), rs)
        y_ref[0] = _silu(g_ref[0]) * (yc * rs * nw_ref[...])


def _ret_call(rq, rkt, rv, rg, ret_decay, norm_w, h0):
    bsz, seq_len, _ = rq.shape
    nc = seq_len // CHUNK
    has_h0 = h0 is not None

    def chunk_of(d, j):
        return jnp.where(d == 0, j, nc - 1 - j)

    tok = pl.BlockSpec((1, CHUNK, D_RET), lambda b, d, j: (b, chunk_of(d, j), 0))
    in_specs = [tok,
                pl.BlockSpec((D_RET, CHUNK), lambda b, d, j: (0, b * nc + chunk_of(d, j))),
                tok, tok,
                pl.BlockSpec((2, H_RET), lambda b, d, j: (0, 0)),
                pl.BlockSpec((1, D_RET), lambda b, d, j: (0, 0))]
    args = [rq, rkt, rv, rg, ret_decay, jnp.tile(norm_w, H_RET).reshape(1, D_RET)]
    y_spec = pl.BlockSpec((1, CHUNK, D_RET), lambda b, d, j: (b, jnp.where(d == 0, nc - 1, nc - 1 - j), 0))
    y_shape = jax.ShapeDtypeStruct((bsz, seq_len, D_RET), F32)
    state_block = (1, 2, H_RET, RET_HEAD_DIM, RET_HEAD_DIM)
    state_spec = pl.BlockSpec(state_block, lambda b, d, j: (b, 0, 0, 0, 0))
    if has_h0:
        in_specs.append(state_spec)
        args.append(h0)
        out_specs, out_shape = y_spec, y_shape
    else:
        out_specs = [y_spec, state_spec]
        out_shape = [y_shape, jax.ShapeDtypeStruct((bsz,) + state_block[1:], F32)]
    return pl.pallas_call(
        functools.partial(_ret_kernel, nc, has_h0),
        grid=(bsz, 2, nc),
        in_specs=in_specs,
        out_specs=out_specs,
        out_shape=out_shape,
        scratch_shapes=[pltpu.VMEM((H_RET // 2, 128, 128), F32),
                        pltpu.VMEM((nc, CHUNK, D_RET), F32)],
        compiler_params=_params(("arbitrary", "arbitrary", "arbitrary")),
        name="ret_latent" if has_h0 else "ret_ctx",
    )(*args)


def _da_kernel(lam_init, q_ref, kt_ref, v_ref, lamp_ref, nw_ref, o_ref):
    lp = lamp_ref[...]
    s1 = jnp.sum(lp[0:1] * lp[1:2], axis=-1, keepdims=True)
    s2 = jnp.sum(lp[2:3] * lp[3:4], axis=-1, keepdims=True)
    lam = jnp.exp(s1) - jnp.exp(s2) + lam_init
    q = q_ref[0]
    v = v_ref[0]
    head_of_lane = lax.shift_right_logical(lax.broadcasted_iota(jnp.int32, (1, D_DA), 1), 6)
    o = jnp.zeros(o_ref.shape[1:], F32)
    for h in range(H_DA):
        e, inv = [], []
        for i in range(2):
            off = DA_HEAD_DIM * h + DA_HALF * i
            s = _dot(q[:, off:off + DA_HALF], kt_ref[0, off:off + DA_HALF, :])
            p = jnp.exp(s - jnp.max(s, axis=-1, keepdims=True))
            e.append(p)
            inv.append(1.0 / jnp.sum(p, axis=-1, keepdims=True))
        att = e[0] * inv[0] - e[1] * (lam * inv[1])
        o = jnp.where(head_of_lane == h, _dot(att.astype(BF16), v), o)
    sq = o * o
    rs = jnp.zeros_like(o)
    for h in range(H_DA):
        ms = jnp.sum(jnp.where(head_of_lane == h, sq, 0.0), axis=-1, keepdims=True) * (1.0 / DA_HEAD_DIM)
        rs = jnp.where(head_of_lane == h, lax.rsqrt(ms + EPS), rs)
    o_ref[0] = o * rs * nw_ref[...] * (1.0 - lam_init)


def _da_call(q, kt, v, lam_params, norm_w, lam_init):
    bsz, lq, _ = q.shape
    lk = kt.shape[2]
    tq = 256
    return pl.pallas_call(
        functools.partial(_da_kernel, lam_init),
        grid=(bsz, lq // tq),
        in_specs=[pl.BlockSpec((1, tq, D_DA), lambda b, i: (b, i, 0)),
                  pl.BlockSpec((1, D_DA, lk), lambda b, i: (b, 0, 0)),
                  pl.BlockSpec((1, lk, D_DA), lambda b, i: (b, 0, 0)),
                  pl.BlockSpec((4, DA_HALF), lambda b, i: (0, 0)),
                  pl.BlockSpec((1, D_DA), lambda b, i: (0, 0))],
        out_specs=pl.BlockSpec((1, tq, D_DA), lambda b, i: (b, i, 0)),
        out_shape=jax.ShapeDtypeStruct((bsz, lq, D_DA), F32),
        compiler_params=_params(("arbitrary", "arbitrary")),
        name="diff_attn",
    )(q, kt, v, lam_params, jnp.tile(norm_w, H_DA).reshape(1, D_DA))


def _out_proj_kernel(ys_ref, yd_ref, yr_ref, x_ref, mod_ref, w_ref, g_ref, b_ref, rw_ref, rb_ref,
                     x1_ref, xm2_ref, comb_ref):
    mix = (_dot(ys_ref[...].astype(BF16), w_ref[0:D_SSD, :])
           + _dot(yd_ref[...].astype(BF16), w_ref[D_SSD:D_SSD + D_DA, :])
           + _dot(yr_ref[...].astype(BF16), w_ref[D_SSD + D_DA:, :]))
    mod = mod_ref[0]
    x1 = _layer_norm(ALPHA * x_ref[...] + mod[2:3] * mix, g_ref[...], b_ref[...])
    x1_ref[...] = x1
    xm2 = x1 * (1.0 + mod[4:5]) + mod[3:4]
    xm2_ref[...] = xm2.astype(BF16)

    logits = _dot(xm2, rw_ref[...], HIGHEST)
    ex = jnp.exp(logits - jnp.max(logits, axis=-1, keepdims=True))
    scores = ex / jnp.sum(ex, axis=-1, keepdims=True)
    sel = scores + rb_ref[...]
    lane = lax.broadcasted_iota(jnp.int32, (1, N_EXPERTS), 1)
    lane_f = lane.astype(F32)
    group_of_lane = lax.shift_right_logical(lane, 2)

    def top2(vals):
        m1 = jnp.max(vals, axis=-1, keepdims=True)
        i1 = jnp.min(jnp.where(vals == m1, lane_f, float(N_EXPERTS)), axis=-1, keepdims=True)
        rest = jnp.where(lane_f == i1, NEG_INF, vals)
        m2 = jnp.max(rest, axis=-1, keepdims=True)
        i2 = jnp.min(jnp.where(rest == m2, lane_f, float(N_EXPERTS)), axis=-1, keepdims=True)
        return m1, i1, m2, i2

    n_groups = N_EXPERTS // EXPERTS_PER_GROUP
    gs = []
    for g in range(n_groups):
        m1, _, m2, _ = top2(jnp.where(group_of_lane == g, sel, NEG_INF))
        gs.append(m1 + m2)
    best = gs[0]
    for g in range(1, n_groups):
        best = jnp.maximum(best, gs[g])
    grp = jnp.full_like(best, n_groups - 1).astype(jnp.int32)
    for g in range(n_groups - 2, -1, -1):
        grp = jnp.where(gs[g] == best, g, grp)
    _, i1, _, i2 = top2(jnp.where(group_of_lane == grp, sel, NEG_INF))
    picked = jnp.where((lane_f == i1) | (lane_f == i2), scores, 0.0)
    comb_ref[...] = picked / jnp.sum(picked, axis=-1, keepdims=True)


def _out_proj_call(y_ssd, y_da, y_ret, x2d, mod, w_out, ln_g, ln_b, router_w, router_b, seq_len):
    n = x2d.shape[0]
    tm = 256
    per_seq = seq_len // tm
    mod_idx = (lambda i: (i // per_seq, 0, 0)) if mod.shape[0] > 1 else (lambda i: (0, 0, 0))
    row = lambda w: pl.BlockSpec((tm, w), lambda i: (i, 0))
    full = lambda s: pl.BlockSpec(s, lambda i: (0,) * len(s))
    return pl.pallas_call(
        _out_proj_kernel,
        grid=(n // tm,),
        in_specs=[row(D_SSD), row(D_DA), row(D_RET), row(D_MODEL),
                  pl.BlockSpec((1, 8, D_MODEL), mod_idx),
                  full((D_MODEL, D_MODEL)), full((1, D_MODEL)), full((1, D_MODEL)),
                  full((D_MODEL, N_EXPERTS)), full((1, N_EXPERTS))],
        out_specs=[row(D_MODEL), row(D_MODEL), row(N_EXPERTS)],
        out_shape=[jax.ShapeDtypeStruct((n, D_MODEL), F32),
                   jax.ShapeDtypeStruct((n, D_MODEL), BF16),
                   jax.ShapeDtypeStruct((n, N_EXPERTS), F32)],
        compiler_params=_params(("arbitrary",)),
        name="out_proj_router",
    )(y_ssd, y_da, y_ret, x2d, mod, w_out, ln_g.reshape(1, D_MODEL), ln_b.reshape(1, D_MODEL),
      router_w, router_b.reshape(1, N_EXPERTS))


def _moe_kernel(xm_ref, comb_ref, wg_ref, wu_ref, wd_ref, x1_ref, mod_ref, g_ref, b_ref, o_ref, acc_ref):
    e = pl.program_id(1)

    @pl.when(e == 0)
    def _():
        acc_ref[...] = jnp.zeros_like(acc_ref)

    xm = xm_ref[...]
    h = _silu(_dot(xm, wg_ref[0])) * _dot(xm, wu_ref[0])
    y = _dot(h.astype(BF16), wd_ref[0])
    lane = lax.broadcasted_iota(jnp.int32, (1, N_EXPERTS), 1)
    w_e = jnp.sum(jnp.where(lane == e, comb_ref[...], 0.0), axis=-1, keepdims=True)
    acc_ref[...] += w_e * y

    @pl.when(e == N_EXPERTS - 1)
    def _():
        mod = mod_ref[0]
        o_ref[...] = _layer_norm(ALPHA * x1_ref[...] + mod[5:6] * acc_ref[...], g_ref[...], b_ref[...])


def _moe_call(xm2, comb, wg, wu, wd, x1, mod, ln_g, ln_b, seq_len):
    n = xm2.shape[0]
    tm = 1024
    per_seq = max(seq_len // tm, 1)
    mod_idx = (lambda i, e: (i // per_seq, 0, 0)) if mod.shape[0] > 1 else (lambda i, e: (0, 0, 0))
    row = lambda w: pl.BlockSpec((tm, w), lambda i, e: (i, 0))
    return pl.pallas_call(
        _moe_kernel,
        grid=(n // tm, N_EXPERTS),
        in_specs=[row(D_MODEL), row(N_EXPERTS),
                  pl.BlockSpec((1, D_MODEL, D_FF), lambda i, e: (e, 0, 0)),
                  pl.BlockSpec((1, D_MODEL, D_FF), lambda i, e: (e, 0, 0)),
                  pl.BlockSpec((1, D_FF, D_MODEL), lambda i, e: (e, 0, 0)),
                  row(D_MODEL),
                  pl.BlockSpec((1, 8, D_MODEL), mod_idx),
                  pl.BlockSpec((1, D_MODEL), lambda i, e: (0, 0)),
                  pl.BlockSpec((1, D_MODEL), lambda i, e: (0, 0))],
        out_specs=row(D_MODEL),
        out_shape=jax.ShapeDtypeStruct((n, D_MODEL), F32),
        scratch_shapes=[pltpu.VMEM((tm, D_MODEL), F32)],
        compiler_params=_params(("arbitrary", "arbitrary")),
        name="moe_dense",
    )(xm2, comb, wg, wu, wd, x1, mod, ln_g.reshape(1, D_MODEL), ln_b.reshape(1, D_MODEL))


def _rot_weight(w, dim):
    nf, m = dim // 4, dim // 2
    blocks = w.reshape(w.shape[0], -1, dim)
    rot = jnp.concatenate([-blocks[..., nf:m], blocks[..., :nf], -blocks[..., m + nf:], blocks[..., m:m + nf]],
                          axis=-1)
    return rot.reshape(w.shape)


def _pack_in_proj(w_in_l, rope):
    z, xbc, dtw, da, ret = jnp.split(w_in_l, (512, 1280, 1296, 2064), axis=1)
    wq, wk, wv = jnp.split(da, 3, axis=1)
    wrq, wrk, wrv, wrg = jnp.split(ret, 4, axis=1)
    dt_pad = jnp.pad(dtw, ((0, 0), (0, C_QROT - C_DT - dtw.shape[1])))
    nn_parts = [z, xbc, wq, wk, wv, wrq, wrv, wrg, dt_pad]
    nt_parts = [wk.T, wrk.T, jnp.pad(dtw.T, ((0, R_KROT - R_DT - dtw.shape[1]), (0, 0)))]
    if rope:
        nn_parts += [_rot_weight(wq, DA_HALF), _rot_weight(wrq, RET_HEAD_DIM)]
        nt_parts += [_rot_weight(wk, DA_HALF).T, _rot_weight(wrk, RET_HEAD_DIM).T]
    return (jnp.concatenate(nn_parts, axis=1).astype(BF16), jnp.concatenate(nt_parts, axis=0).astype(BF16))


def _rope_tables(seq_len, dim, reps):
    nf = dim // 4
    inv = ROPE_BASE ** (-jnp.arange(nf, dtype=F32) / nf)
    t = jnp.arange(seq_len)
    r = (t // GRID_W).astype(F32)[:, None] * inv
    c = (t % GRID_W).astype(F32)[:, None] * inv
    ang = jnp.tile(jnp.concatenate([r, r, c, c], axis=-1), (1, reps))
    return jnp.cos(ang), jnp.sin(ang)


def _layer(x, mod, l, P, ctx, rope_tabs):
    bsz, seq_len, _ = x.shape
    n = bsz * seq_len
    x2d = x.reshape(n, D_MODEL)
    latent = ctx is not None
    wnn, wnt = P["in_proj"][l][1 if latent else 0]
    outs = _in_proj_call(x2d, mod, wnn, wnt, seq_len, rope_tabs)
    z, xbc, dt, dtt, q, kt, v, rq, rkt, rv, rg = outs[:11]
    r3 = lambda a: a.reshape(bsz, seq_len, a.shape[-1])

    h_ssd0 = ctx[2] if latent else None
    h_ret0 = ctx[3] if latent else None
    ssd_out = _ssd_call(r3(xbc), r3(z), r3(dt), dtt, P["ssd_conv_w"][l], P["ssd_conv_b"][l],
                        P["ssd_dt_bias"][l], P["ssd_a_log"][l], P["ssd_d"][l], P["ssd_norm_w"][l], h_ssd0)
    ret_out = _ret_call(r3(rq), rkt, r3(rv), r3(rg), P["ret_decay"][l], P["ret_norm_w"][l], h_ret0)

    kt_b = jnp.transpose(kt.reshape(D_DA, bsz, seq_len), (1, 0, 2))
    v_b = r3(v)
    if latent:
        y_ssd, y_ret = ssd_out, ret_out
        new = None
        cache_kt = jnp.transpose(ctx[0].reshape(bsz, -1, D_DA), (0, 2, 1)).astype(BF16)
        kt_b = jnp.concatenate([kt_b, cache_kt], axis=2)
        v_b = jnp.concatenate([v_b, ctx[1].reshape(bsz, -1, D_DA)], axis=1)
    else:
        y_ssd, hs = ssd_out
        y_ret, hr = ret_out
        k = outs[11]
        new = (k.reshape(bsz, seq_len, H_DA, DA_HEAD_DIM), v.reshape(bsz, seq_len, H_DA, DA_HEAD_DIM), hs, hr)
    lam_init = 0.8 - 0.6 * math.exp(-0.3 * l)
    y_da = _da_call(r3(q), kt_b, v_b.astype(BF16), P["da_lambda"][l], P["da_norm_w"][l], lam_init)

    x1, xm2, comb = _out_proj_call(y_ssd.reshape(n, D_SSD), y_da.reshape(n, D_DA), y_ret.reshape(n, D_RET),
                                   x2d, mod, P["w_out16"][l], P["ln_mix_g"][l], P["ln_mix_b"][l],
                                   P["router_w"], P["router_b"], seq_len)
    x2 = _moe_call(xm2, comb, P["wg16"][l], P["wu16"][l], P["wd16"][l], x1, mod,
                   P["ln_ffn_g"][l], P["ln_ffn_b"][l], seq_len)
    return x2.reshape(bsz, seq_len, D_MODEL), new


def kernel(x_prompt, x_sample, c, cache_da_k, cache_da_v, state_ssd, state_ret, c_ctx, w_ada, b_ada, w_in,
           ssd_conv_w, ssd_conv_b, ssd_dt_bias, ssd_a_log, ssd_d, ssd_norm_w, da_lambda, da_norm_w, ret_decay,
           ret_norm_w, w_out, ln_mix_g, ln_mix_b, router_w, router_b, moe_w_gate, moe_w_up, moe_w_down,
           ln_ffn_g, ln_ffn_b):
    dec_b = x_sample.shape[0]
    P = dict(ssd_conv_w=ssd_conv_w, ssd_conv_b=ssd_conv_b, ssd_dt_bias=ssd_dt_bias, ssd_a_log=ssd_a_log,
             ssd_d=ssd_d, ssd_norm_w=ssd_norm_w, da_lambda=da_lambda, da_norm_w=da_norm_w,
             ret_decay=ret_decay, ret_norm_w=ret_norm_w, ln_mix_g=ln_mix_g, ln_mix_b=ln_mix_b,
             router_w=router_w, router_b=router_b, ln_ffn_g=ln_ffn_g, ln_ffn_b=ln_ffn_b)
    P["in_proj"] = [(_pack_in_proj(w_in[l], False), _pack_in_proj(w_in[l], True)) for l in range(DEPTH)]
    P["w_out16"] = w_out.astype(BF16)
    P["wg16"] = moe_w_gate.astype(BF16)
    P["wu16"] = moe_w_up.astype(BF16)
    P["wd16"] = moe_w_down.astype(BF16)

    cond8 = jnp.concatenate([c_ctx[None, :], c, jnp.zeros((8 - 1 - dec_b, D_MODEL), F32)], axis=0)
    mod_all = _ada_call(cond8, w_ada, b_ada).reshape(DEPTH, 8, 6, D_MODEL)
    mod_all = jnp.pad(mod_all, ((0, 0), (0, 0), (0, 2), (0, 0)))

    y = x_prompt
    ks_, vs_, hs_, hr_ = [], [], [], []
    for l in range(DEPTH):
        y, (k_l, v_l, hs_l, hr_l) = _layer(y, mod_all[l, 0:1], l, P, None, None)
        ks_.append(k_l)
        vs_.append(v_l)
        hs_.append(hs_l)
        hr_.append(hr_l)
    new_da_k = jnp.stack(ks_, axis=1)
    new_da_v = jnp.stack(vs_, axis=1)
    new_ssd = jnp.stack(hs_, axis=1)
    new_ret = jnp.stack(hr_, axis=1)

    seq_len = x_sample.shape[1]
    cos_da, sin_da = _rope_tables(seq_len, DA_HALF, D_DA // DA_HALF)
    cos_ret, sin_ret = _rope_tables(seq_len, RET_HEAD_DIM, H_RET)
    rope_tabs = (cos_da, sin_da, cos_ret, sin_ret, cos_da.T, sin_da.T, cos_ret.T, sin_ret.T)
    zl = x_sample
    for l in range(DEPTH):
        ctx = (cache_da_k[:, l], cache_da_v[:, l], state_ssd[:, l], state_ret[:, l])
        zl, _ = _layer(zl, mod_all[l, 1:1 + dec_b], l, P, ctx, rope_tabs)

    return (y, zl, new_da_k, new_da_v, new_ssd, new_ret)
```

```python
import functools
import math

import jax
import jax.numpy as jnp
from jax import lax
from jax.experimental import pallas as pl
from jax.experimental.pallas import tpu as pltpu

D_MODEL = 1024
DEPTH = 2
GRID_W = 64
CHUNK = 128
H_SSD = 8
SSD_HEAD_DIM = 64
D_SSD = H_SSD * SSD_HEAD_DIM
SSD_GROUPS = 2
D_STATE = 64
CONV_W = 5
D_XBC = D_SSD + 2 * SSD_GROUPS * D_STATE
H_DA = 4
DA_HALF = 32
DA_HEAD_DIM = 2 * DA_HALF
D_DA = H_DA * DA_HEAD_DIM
H_RET = 4
RET_HEAD_DIM = 64
D_RET = H_RET * RET_HEAD_DIM
N_EXPERTS = 16
EXPERTS_PER_GROUP = 4
N_GROUPS = N_EXPERTS // EXPERTS_PER_GROUP
D_FF = 512
ROPE_BASE = 10000.0
EPS = 1e-6
ALPHA = (2 * DEPTH) ** 0.25

F32 = jnp.float32
BF16 = jnp.bfloat16
HIGHEST = lax.Precision.HIGHEST
NEG_INF = float("-inf")
LOG2_E = 1.4426950408889634

V7X_VMEM_LIMIT_BYTES = 56 * 1024 * 1024
CONV_HALO = 8

C_Z, C_XBC, C_Q, C_K, C_V, C_RQ, C_RV, C_RG, C_DT, C_QROT, C_RQROT, C_END = (
    0, 512, 1280, 1536, 1792, 2048, 2304, 2560, 2816, 2944, 3200, 3456)
R_K, R_RK, R_DT, R_KROT, R_RKROT, R_END = 0, 256, 512, 544, 800, 1056


def _silu(x):
    return x * (1.0 / (1.0 + jnp.exp(-x)))


def _softplus(x):
    return jnp.maximum(x, 0.0) + jnp.log1p(jnp.exp(-jnp.abs(x)))


def _dot(a, b, precision=None):
    return jnp.dot(a, b, preferred_element_type=F32, precision=precision)


def _layer_norm(t, g, b):
    mu = jnp.mean(t, axis=-1, keepdims=True)
    tc = t - mu
    var = jnp.mean(tc * tc, axis=-1, keepdims=True)
    return tc * lax.rsqrt(var + EPS) * g + b


def _params(sem):
    return pltpu.CompilerParams(dimension_semantics=sem, vmem_limit_bytes=V7X_VMEM_LIMIT_BYTES)


def _ada_kernel(cond_ref, w_ref, b_ref, o_ref):
    o_ref[0] = _dot(_silu(cond_ref[...]), w_ref[0], HIGHEST) + b_ref[0]


def _ada_call(cond8, w_ada, b_ada):
    tn = 1536
    nb = (6 * D_MODEL) // tn
    return pl.pallas_call(
        _ada_kernel,
        grid=(DEPTH, nb),
        in_specs=[pl.BlockSpec((8, D_MODEL), lambda l, n: (0, 0)),
                  pl.BlockSpec((1, D_MODEL, tn), lambda l, n: (l, 0, n)),
                  pl.BlockSpec((1, 1, tn), lambda l, n: (l, 0, n))],
        out_specs=pl.BlockSpec((1, 8, tn), lambda l, n: (l, 0, n)),
        out_shape=jax.ShapeDtypeStruct((DEPTH, 8, 6 * D_MODEL), F32),
        compiler_params=_params(("arbitrary", "arbitrary")),
        name="ada_mod",
    )(cond8, w_ada, b_ada.reshape(DEPTH, 1, 6 * D_MODEL))


def _in_proj_kernel(rope, *refs):
    if rope:
        (x_ref, mod_ref, wnn_ref, wnt_ref, cd_ref, sd_ref, cr_ref, sr_ref,
         cdt_ref, sdt_ref, crt_ref, srt_ref,
         z_ref, xbc_ref, dt_ref, dtt_ref, q_ref, kt_ref, v_ref, rq_ref, rkt_ref, rv_ref, rg_ref) = refs
    else:
        (x_ref, mod_ref, wnn_ref, wnt_ref,
         z_ref, xbc_ref, dt_ref, dtt_ref, q_ref, kt_ref, v_ref, rq_ref, rkt_ref, rv_ref, rg_ref,
         k_ref) = refs
    mod = mod_ref[0]
    xm = (x_ref[...] * (1.0 + mod[1:2]) + mod[0:1]).astype(BF16)

    def nn(a, b):
        return _dot(xm, wnn_ref[:, a:b])

    def nt(a, b):
        return lax.dot_general(wnt_ref[a:b, :], xm, (((1,), (1,)), ((), ())),
                               preferred_element_type=F32)

    z_ref[...] = nn(C_Z, C_XBC)
    xbc_ref[...] = nn(C_XBC, C_Q)
    dt_ref[...] = nn(C_DT, C_QROT)[:, :2 * H_SSD]
    dtt_ref[...] = nt(R_DT, R_KROT)[:2 * H_SSD, :]
    v_ref[...] = nn(C_V, C_RQ)
    rv_ref[...] = nn(C_RV, C_RG)
    rg_ref[...] = nn(C_RG, C_DT)
    q = nn(C_Q, C_K)
    rq = nn(C_RQ, C_RV)
    kt = nt(R_K, R_RK)
    rkt = nt(R_RK, R_DT)
    if rope:
        q = q * cd_ref[...] + nn(C_QROT, C_RQROT) * sd_ref[...]
        rq = rq * cr_ref[...] + nn(C_RQROT, C_END) * sr_ref[...]
        kt = kt * cdt_ref[...] + nt(R_KROT, R_RKROT) * sdt_ref[...]
        rkt = rkt * crt_ref[...] + nt(R_RKROT, R_END) * srt_ref[...]
    else:
        k_ref[...] = nn(C_K, C_V)
    q_ref[...] = (q * (DA_HALF ** -0.5 * LOG2_E)).astype(BF16)
    rq_ref[...] = (rq * (RET_HEAD_DIM ** -0.5)).astype(BF16)
    kt_ref[...] = kt.astype(BF16)
    rkt_ref[...] = rkt.astype(BF16)


def _in_proj_call(x2d, mod, wnn, wnt, seq_len, rope_tabs):
    n = x2d.shape[0]
    tm = 256
    nblk = n // tm
    per_seq = seq_len // tm
    n_mod = mod.shape[0]
    rope = rope_tabs is not None
    mod_idx = (lambda i: (i // per_seq, 0, 0)) if n_mod > 1 else (lambda i: (0, 0, 0))
    row = lambda w: pl.BlockSpec((tm, w), lambda i: (i, 0))
    col = lambda h: pl.BlockSpec((h, tm), lambda i: (0, i))
    in_specs = [row(D_MODEL),
                pl.BlockSpec((1, 8, D_MODEL), mod_idx),
                pl.BlockSpec(wnn.shape, lambda i: (0, 0)),
                pl.BlockSpec(wnt.shape, lambda i: (0, 0))]
    args = [x2d, mod, wnn, wnt]
    if rope:
        in_specs += [pl.BlockSpec((tm, 256), lambda i: (i % per_seq, 0))] * 4
        in_specs += [pl.BlockSpec((256, tm), lambda i: (0, i % per_seq))] * 4
        args += list(rope_tabs)
    sds = jax.ShapeDtypeStruct
    out_shape = [sds((n, D_SSD), F32), sds((n, D_XBC), F32), sds((n, 2 * H_SSD), F32),
                 sds((2 * H_SSD, n), F32), sds((n, D_DA), BF16), sds((D_DA, n), BF16),
                 sds((n, D_DA), F32), sds((n, D_RET), BF16), sds((D_RET, n), BF16),
                 sds((n, D_RET), F32), sds((n, D_RET), F32)]
    out_specs = [row(D_SSD), row(D_XBC), row(2 * H_SSD), col(2 * H_SSD), row(D_DA), col(D_DA),
                 row(D_DA), row(D_RET), col(D_RET), row(D_RET), row(D_RET)]
    if not rope:
        out_shape.append(sds((n, D_DA), F32))
        out_specs.append(row(D_DA))
    return pl.pallas_call(
        functools.partial(_in_proj_kernel, rope),
        grid=(nblk,),
        in_specs=in_specs,
        out_specs=out_specs,
        out_shape=out_shape,
        compiler_params=_params(("arbitrary",)),
        name="in_proj_rope" if rope else "in_proj",
    )(*args)


def _ssd_kernel(nc, has_h0, *refs):
    refs = list(refs)
    (xm_ref, xp_ref, xn_ref, z_ref, dt_ref, dtt_ref, cw_ref, cb_ref, dtb_c_ref, dtb_r_ref,
     alog_c_ref, alog_r_ref, dskip_ref, nw_ref) = refs[:14]
    refs = refs[14:]
    h0_ref = refs.pop(0) if has_h0 else None
    y_ref = refs.pop(0)
    hs_ref = None if has_h0 else refs.pop(0)
    ext_ref, hp_ref, yf_ref = refs

    d = pl.program_id(1)
    j = pl.program_id(2)
    fwd = d == 0
    cj = jnp.where(fwd, j, nc - 1 - j)

    ext_ref[0:CONV_HALO, :] = jnp.where(cj > 0, xp_ref[0], 0.0)
    ext_ref[CONV_HALO:CONV_HALO + CHUNK, :] = xm_ref[0]
    ext_ref[CONV_HALO + CHUNK:, :] = jnp.where(cj < nc - 1, xn_ref[0], 0.0)
    acc = jnp.zeros((CHUNK, D_XBC), F32) + cb_ref[...]
    for k in range(CONV_W):
        start = CONV_HALO - CONV_W // 2 + k
        acc = acc + cw_ref[k:k + 1, :] * ext_ref[start:start + CHUNK, :]
    act = _silu(acc)
    xs = act[:, :D_SSD]
    bm = act[:, D_SSD:D_SSD + 128]
    cm = act[:, D_SSD + 128:]

    dt_c_all = _softplus(dt_ref[0] + dtb_c_ref[...])
    dt_r_all = _softplus(dtt_ref[...] + dtb_r_ref[...])
    a_c_all = -jnp.exp(alog_c_ref[...])
    a_r_all = -jnp.exp(alog_r_ref[...])
    dt_c = jnp.where(fwd, dt_c_all[:, :H_SSD], dt_c_all[:, H_SSD:])
    dt_r = jnp.where(fwd, dt_r_all[:H_SSD], dt_r_all[H_SSD:])
    la_c = dt_c * jnp.where(fwd, a_c_all[:, :H_SSD], a_c_all[:, H_SSD:])
    la_r = dt_r * jnp.where(fwd, a_r_all[:H_SSD], a_r_all[H_SSD:])

    ii = lax.broadcasted_iota(jnp.int32, (CHUNK, CHUNK), 0)
    jj = lax.broadcasted_iota(jnp.int32, (CHUNK, CHUNK), 1)
    valid = jnp.where(fwd, ii - jj, jj - ii) >= 0
    valid_t = jnp.where(fwd, jj - ii, ii - jj) >= 0
    cum_c = _dot(valid.astype(F32), la_c, HIGHEST)
    cum_r = _dot(la_r, valid_t.astype(F32), HIGHEST)
    tot_r = jnp.where(fwd, cum_r[:, CHUNK - 1:], cum_r[:, :1])
    e_tot = jnp.exp(tot_r)
    wk_r = dt_r * jnp.exp(tot_r - cum_r)
    e_cum_c = jnp.exp(cum_c)

    bm_t = bm.T
    bm_t16 = bm_t.astype(BF16)
    lane = lax.broadcasted_iota(jnp.int32, (1, 128), 1)
    low_lanes = lane < 64
    feat_row = lax.broadcasted_iota(jnp.int32, (128, 1), 0)

    @pl.when(j == 0)
    def _():
        for p in range(H_SSD // 2):
            if has_h0:
                g = p // 2
                both = jnp.concatenate([h0_ref[0, pl.ds(d, 1), 2 * p][0],
                                        h0_ref[0, pl.ds(d, 1), 2 * p + 1][0]], axis=1)
                zero = jnp.zeros((D_STATE, 128), F32)
                hp_ref[p] = jnp.concatenate([both, zero] if g == 0 else [zero, both], axis=0)
            else:
                hp_ref[p] = jnp.zeros((128, 128), F32)

    ys = []
    for g in range(SSD_GROUPS):
        in_group_lane = (lane >= 64 * g) & (lane < 64 * (g + 1))
        cg = jnp.where(in_group_lane, cm, 0.0)
        gram = _dot(cg.astype(BF16), bm_t16)
        in_group_row = (feat_row >= 64 * g) & (feat_row < 64 * (g + 1))
        for p in range(2 * g, 2 * g + 2):
            xs_p = xs[:, 128 * p:128 * (p + 1)].astype(BF16)
            h_prev = hp_ref[p]
            h_prev16 = h_prev.astype(BF16)
            y_h, s_h = [], []
            for hh in (2 * p, 2 * p + 1):
                decay = jnp.exp(jnp.where(valid, cum_c[:, hh:hh + 1] - cum_r[hh:hh + 1, :], NEG_INF))
                sc = gram * decay * dt_r[hh:hh + 1, :]
                y_intra = _dot(sc.astype(BF16), xs_p)
                y_inter = _dot((cg * e_cum_c[:, hh:hh + 1]).astype(BF16), h_prev16)
                y_h.append(y_intra + y_inter)
                s_h.append(_dot((bm_t * wk_r[hh:hh + 1, :]).astype(BF16), xs_p))
            ys.append(jnp.where(low_lanes, y_h[0], y_h[1]))
            s_pair = jnp.where(low_lanes, s_h[0], s_h[1])
            e_pair = jnp.where(low_lanes, e_tot[2 * p:2 * p + 1, :], e_tot[2 * p + 1:2 * p + 2, :])
            hp_ref[p] = jnp.where(in_group_row, e_pair * h_prev + s_pair, 0.0)
    y_dir = jnp.concatenate(ys, axis=1)

    if hs_ref is not None:
        @pl.when(j == nc - 1)
        def _():
            for dd in range(2):
                @pl.when(d == dd)
                def _():
                    for p in range(H_SSD // 2):
                        g = p // 2
                        hn = hp_ref[p]
                        hs_ref[0, dd, 2 * p] = hn[64 * g:64 * (g + 1), :64]
                        hs_ref[0, dd, 2 * p + 1] = hn[64 * g:64 * (g + 1), 64:]

    @pl.when(fwd)
    def _():
        yf_ref[cj] = y_dir

    @pl.when(d == 1)
    def _():
        yt = yf_ref[cj] + y_dir + dskip_ref[...] * xs
        gated = yt * _silu(z_ref[0])
        ms = jnp.mean(gated * gated, axis=-1, keepdims=True)
        y_ref[0] = gated * lax.rsqrt(ms + EPS) * nw_ref[...]


def _ssd_call(xbc, z, dt, dtt, conv_w, conv_b, dt_bias, a_log, d_skip, norm_w, h0):
    bsz, seq_len, _ = xbc.shape
    nc = seq_len // CHUNK
    has_h0 = h0 is not None
    hb = CHUNK // CONV_HALO
    last_hb = seq_len // CONV_HALO - 1

    def chunk_of(d, j):
        return jnp.where(d == 0, j, nc - 1 - j)

    def full(shape):
        return pl.BlockSpec(shape, lambda b, d, j: (0,) * len(shape))

    in_specs = [
        pl.BlockSpec((1, CHUNK, D_XBC), lambda b, d, j: (b, chunk_of(d, j), 0)),
        pl.BlockSpec((1, CONV_HALO, D_XBC), lambda b, d, j: (b, jnp.maximum(chunk_of(d, j) * hb - 1, 0), 0)),
        pl.BlockSpec((1, CONV_HALO, D_XBC),
                     lambda b, d, j: (b, jnp.minimum((chunk_of(d, j) + 1) * hb, last_hb), 0)),
        pl.BlockSpec((1, CHUNK, D_SSD), lambda b, d, j: (b, chunk_of(d, j), 0)),
        pl.BlockSpec((1, CHUNK, 2 * H_SSD), lambda b, d, j: (b, chunk_of(d, j), 0)),
        pl.BlockSpec((2 * H_SSD, CHUNK), lambda b, d, j: (0, b * nc + chunk_of(d, j))),
        full((CONV_W, D_XBC)), full((1, D_XBC)), full((1, 2 * H_SSD)), full((2 * H_SSD, 1)),
        full((1, 2 * H_SSD)), full((2 * H_SSD, 1)), full((1, D_SSD)), full((1, D_SSD)),
    ]
    args = [xbc, xbc, xbc, z, dt, dtt, conv_w, conv_b.reshape(1, D_XBC),
            dt_bias.reshape(1, 2 * H_SSD), dt_bias.reshape(2 * H_SSD, 1),
            a_log.reshape(1, 2 * H_SSD), a_log.reshape(2 * H_SSD, 1),
            jnp.repeat(d_skip, SSD_HEAD_DIM).reshape(1, D_SSD), norm_w.reshape(1, D_SSD)]
    y_spec = pl.BlockSpec((1, CHUNK, D_SSD), lambda b, d, j: (b, jnp.where(d == 0, nc - 1, nc - 1 - j), 0))
    y_shape = jax.ShapeDtypeStruct((bsz, seq_len, D_SSD), F32)
    state_block = (1, 2, H_SSD, D_STATE, SSD_HEAD_DIM)
    state_spec = pl.BlockSpec(state_block, lambda b, d, j: (b, 0, 0, 0, 0))
    if has_h0:
        in_specs.append(state_spec)
        args.append(h0)
        out_specs, out_shape = y_spec, y_shape
    else:
        out_specs = [y_spec, state_spec]
        out_shape = [y_shape, jax.ShapeDtypeStruct((bsz,) + state_block[1:], F32)]
    return pl.pallas_call(
        functools.partial(_ssd_kernel, nc, has_h0),
        grid=(bsz, 2, nc),
        in_specs=in_specs,
        out_specs=out_specs,
        out_shape=out_shape,
        scratch_shapes=[pltpu.VMEM((CHUNK + 2 * CONV_HALO, D_XBC), F32),
                        pltpu.VMEM((H_SSD // 2, 128, 128), F32),
                        pltpu.VMEM((nc, CHUNK, D_SSD), F32)],
        compiler_params=_params(("arbitrary", "arbitrary", "arbitrary")),
        name="ssd_latent" if has_h0 else "ssd_ctx",
    )(*args)


def _ret_kernel(nc, has_h0, *refs):
    refs = list(refs)
    q_ref, kt_ref, v_ref, g_ref, dec_ref, nw_ref = refs[:6]
    refs = refs[6:]
    h0_ref = refs.pop(0) if has_h0 else None
    y_ref = refs.pop(0)
    hs_ref = None if has_h0 else refs.pop(0)
    hr_ref, yf_ref = refs

    d = pl.program_id(1)
    j = pl.program_id(2)
    fwd = d == 0
    cj = jnp.where(fwd, j, nc - 1 - j)

    dec = dec_ref[...]
    lg_all = -_softplus(-dec)
    lg = jnp.where(fwd, lg_all[0:1], lg_all[1:2])

    ii = lax.broadcasted_iota(jnp.int32, (CHUNK, CHUNK), 0)
    jj = lax.broadcasted_iota(jnp.int32, (CHUNK, CHUNK), 1)
    dist = jnp.where(fwd, ii - jj, jj - ii)
    valid = dist >= 0
    dist_f = dist.astype(F32)
    qpos = lax.broadcasted_iota(jnp.int32, (CHUNK, 1), 0)
    kpos = lax.broadcasted_iota(jnp.int32, (1, CHUNK), 1)
    n_q = jnp.where(fwd, qpos + 1, CHUNK - qpos).astype(F32)
    n_k = jnp.where(fwd, CHUNK - 1 - kpos, kpos).astype(F32)
    lane = lax.broadcasted_iota(jnp.int32, (1, 128), 1)
    low_lanes = lane < 64
    low_rows = lax.broadcasted_iota(jnp.int32, (128, 1), 0) < 64
    block_diag = low_rows == low_lanes

    @pl.when(j == 0)
    def _():
        for p in range(H_RET // 2):
            if has_h0:
                zero = jnp.zeros((RET_HEAD_DIM, RET_HEAD_DIM), F32)
                top = jnp.concatenate([h0_ref[0, pl.ds(d, 1), 2 * p][0], zero], axis=1)
                bot = jnp.concatenate([zero, h0_ref[0, pl.ds(d, 1), 2 * p + 1][0]], axis=1)
                hr_ref[p] = jnp.concatenate([top, bot], axis=0)
            else:
                hr_ref[p] = jnp.zeros((128, 128), F32)

    ys = []
    for p in range(H_RET // 2):
        q_p = q_ref[0, :, 128 * p:128 * (p + 1)].astype(F32)
        kt_p = kt_ref[128 * p:128 * (p + 1), :]
        v_p = v_ref[0, :, 128 * p:128 * (p + 1)].astype(BF16)
        h_prev = hr_ref[p]
        lg_a = lg[:, 2 * p:2 * p + 1]
        lg_b = lg[:, 2 * p + 1:2 * p + 2]
        y_h = []
        for lg_h, in_head in ((lg_a, low_lanes), (lg_b, jnp.logical_not(low_lanes))):
            s = _dot(jnp.where(in_head, q_p, 0.0).astype(BF16), kt_p)
            decay = jnp.exp(jnp.where(valid, dist_f * lg_h, NEG_INF))
            y_h.append(_dot((s * decay).astype(BF16), v_p))
        e_q = jnp.where(low_lanes, jnp.exp(n_q * lg_a), jnp.exp(n_q * lg_b))
        y_inter = _dot((q_p * e_q).astype(BF16), h_prev.astype(BF16))
        w_k = jnp.where(low_rows, jnp.exp(n_k * lg_a), jnp.exp(n_k * lg_b))
        s_new = _dot((kt_p.astype(F32) * w_k).astype(BF16), v_p)
        e_tot = jnp.where(low_lanes, jnp.exp(CHUNK * lg_a), jnp.exp(CHUNK * lg_b))
        hr_ref[p] = e_tot * h_prev + jnp.where(block_diag, s_new, 0.0)
        ys.append(jnp.where(low_lanes, y_h[0], y_h[1]) + y_inter)
    y_dir = jnp.concatenate(ys, axis=1)

    if hs_ref is not None:
        @pl.when(j == nc - 1)
        def _():
            for dd in range(2):
                @pl.when(d == dd)
                def _():
                    for p in range(H_RET // 2):
                        hn = hr_ref[p]
                        hs_ref[0, dd, 2 * p] = hn[:64, :64]
                        hs_ref[0, dd, 2 * p + 1] = hn[64:, 64:]

    @pl.when(fwd)
    def _():
        yf_ref[cj] = y_dir

    @pl.when(d == 1)
    def _():
        yt = yf_ref[cj] + y_dir
        lane4 = lax.shift_right_logical(lax.broadcasted_iota(jnp.int32, (1, D_RET), 1), 6)
        mu = jnp.zeros_like(yt)
        for h in range(H_RET):
            m_h = jnp.sum(jnp.where(lane4 == h, yt, 0.0), axis=-1, keepdims=True) * (1.0 / RET_HEAD_DIM)
            mu = jnp.where(lane4 == h, m_h, mu)
        yc = yt - mu
        sq = yc * yc
        rs = jnp.zeros_like(yt)
        for h in range(H_RET):
            v_h = jnp.sum(jnp.where(lane4 == h, sq, 0.0), axis=-1, keepdims=True) * (1.0 / RET_HEAD_DIM)
            rs = jnp.where(lane4 == h, lax.rsqrt(v_h + EPS), rs)
        y_ref[0] = _silu(g_ref[0]) * (yc * rs * nw_ref[...])


def _ret_call(rq, rkt, rv, rg, ret_decay, norm_w, h0):
    bsz, seq_len, _ = rq.shape
    nc = seq_len // CHUNK
    has_h0 = h0 is not None

    def chunk_of(d, j):
        return jnp.where(d == 0, j, nc - 1 - j)

    tok = pl.BlockSpec((1, CHUNK, D_RET), lambda b, d, j: (b, chunk_of(d, j), 0))
    in_specs = [tok,
                pl.BlockSpec((D_RET, CHUNK), lambda b, d, j: (0, b * nc + chunk_of(d, j))),
                tok, tok,
                pl.BlockSpec((2, H_RET), lambda b, d, j: (0, 0)),
                pl.BlockSpec((1, D_RET), lambda b, d, j: (0, 0))]
    args = [rq, rkt, rv, rg, ret_decay, jnp.tile(norm_w, H_RET).reshape(1, D_RET)]
    y_spec = pl.BlockSpec((1, CHUNK, D_RET), lambda b, d, j: (b, jnp.where(d == 0, nc - 1, nc - 1 - j), 0))
    y_shape = jax.ShapeDtypeStruct((bsz, seq_len, D_RET), F32)
    state_block = (1, 2, H_RET, RET_HEAD_DIM, RET_HEAD_DIM)
    state_spec = pl.BlockSpec(state_block, lambda b, d, j: (b, 0, 0, 0, 0))
    if has_h0:
        in_specs.append(state_spec)
        args.append(h0)
        out_specs, out_shape = y_spec, y_shape
    else:
        out_specs = [y_spec, state_spec]
        out_shape = [y_shape, jax.ShapeDtypeStruct((bsz,) + state_block[1:], F32)]
    return pl.pallas_call(
        functools.partial(_ret_kernel, nc, has_h0),
        grid=(bsz, 2, nc),
        in_specs=in_specs,
        out_specs=out_specs,
        out_shape=out_shape,
        scratch_shapes=[pltpu.VMEM((H_RET // 2, 128, 128), F32),
                        pltpu.VMEM((nc, CHUNK, D_RET), F32)],
        compiler_params=_params(("arbitrary", "arbitrary", "arbitrary")),
        name="ret_latent" if has_h0 else "ret_ctx",
    )(*args)


def _da_kernel(lam_init, q_ref, kt_ref, v_ref, lamp_ref, nw_ref, o_ref):
    lp = lamp_ref[...]
    s1 = jnp.sum(lp[0:1] * lp[1:2], axis=-1, keepdims=True)
    s2 = jnp.sum(lp[2:3] * lp[3:4], axis=-1, keepdims=True)
    lam = jnp.exp(s1) - jnp.exp(s2) + lam_init
    q = q_ref[0]
    v = v_ref[0]
    head_of_lane = lax.shift_right_logical(lax.broadcasted_iota(jnp.int32, (1, D_DA), 1), 6)
    o = jnp.zeros(o_ref.shape[1:], F32)
    for h in range(H_DA):
        pv, inv = [], []
        for i in range(2):
            off = DA_HEAD_DIM * h + DA_HALF * i
            s = _dot(q[:, off:off + DA_HALF], kt_ref[0, off:off + DA_HALF, :])
            p = jnp.exp2(s - jnp.max(s, axis=-1, keepdims=True))
            inv.append(1.0 / jnp.sum(p, axis=-1, keepdims=True))
            pv.append(_dot(p.astype(BF16), v))
        o_h = pv[0] * inv[0] - pv[1] * (lam * inv[1])
        o = jnp.where(head_of_lane == h, o_h, o)
    sq = o * o
    rs = jnp.zeros_like(o)
    for h in range(H_DA):
        ms = jnp.sum(jnp.where(head_of_lane == h, sq, 0.0), axis=-1, keepdims=True) * (1.0 / DA_HEAD_DIM)
        rs = jnp.where(head_of_lane == h, lax.rsqrt(ms + EPS), rs)
    o_ref[0] = o * rs * nw_ref[...] * (1.0 - lam_init)


def _da_call(q, kt, v, lam_params, norm_w, lam_init):
    bsz, lq, _ = q.shape
    lk = kt.shape[2]
    tq = 256
    return pl.pallas_call(
        functools.partial(_da_kernel, lam_init),
        grid=(bsz, lq // tq),
        in_specs=[pl.BlockSpec((1, tq, D_DA), lambda b, i: (b, i, 0)),
                  pl.BlockSpec((1, D_DA, lk), lambda b, i: (b, 0, 0)),
                  pl.BlockSpec((1, lk, D_DA), lambda b, i: (b, 0, 0)),
                  pl.BlockSpec((4, DA_HALF), lambda b, i: (0, 0)),
                  pl.BlockSpec((1, D_DA), lambda b, i: (0, 0))],
        out_specs=pl.BlockSpec((1, tq, D_DA), lambda b, i: (b, i, 0)),
        out_shape=jax.ShapeDtypeStruct((bsz, lq, D_DA), F32),
        compiler_params=_params(("arbitrary", "arbitrary")),
        name="diff_attn",
    )(q, kt, v, lam_params, jnp.tile(norm_w, H_DA).reshape(1, D_DA))


def _out_proj_kernel(ys_ref, yd_ref, yr_ref, x_ref, mod_ref, w_ref, g_ref, b_ref, rwt_ref, rb_ref,
                     x1_ref, xm2_ref, combt_ref, grp_ref):
    mix = (_dot(ys_ref[...].astype(BF16), w_ref[0:D_SSD, :])
           + _dot(yd_ref[...].astype(BF16), w_ref[D_SSD:D_SSD + D_DA, :])
           + _dot(yr_ref[...].astype(BF16), w_ref[D_SSD + D_DA:, :]))
    mod = mod_ref[0]
    x1 = _layer_norm(ALPHA * x_ref[...] + mod[2:3] * mix, g_ref[...], b_ref[...])
    x1_ref[...] = x1
    xm2 = x1 * (1.0 + mod[4:5]) + mod[3:4]
    xm2_ref[...] = xm2.astype(BF16)

    logits = lax.dot_general(rwt_ref[...], xm2, (((1,), (1,)), ((), ())),
                             preferred_element_type=F32, precision=HIGHEST)
    ex = jnp.exp(logits - jnp.max(logits, axis=0, keepdims=True))
    scores = ex / jnp.sum(ex, axis=0, keepdims=True)
    sel = scores + rb_ref[...]
    row = lax.broadcasted_iota(jnp.int32, sel.shape, 0)
    row_f = row.astype(F32)
    gs = []
    for g in range(N_GROUPS):
        v = [sel[EXPERTS_PER_GROUP * g + i:EXPERTS_PER_GROUP * g + i + 1, :] for i in range(EXPERTS_PER_GROUP)]
        pair_sums = [v[a] + v[b] for a in range(EXPERTS_PER_GROUP) for b in range(a + 1, EXPERTS_PER_GROUP)]
        gs.append(functools.reduce(jnp.maximum, pair_sums))
    best = functools.reduce(jnp.maximum, gs)
    grp = jnp.full(best.shape, N_GROUPS - 1, jnp.int32)
    for g in range(N_GROUPS - 2, -1, -1):
        grp = jnp.where(gs[g] == best, g, grp)
    masked = jnp.where(lax.shift_right_logical(row, 2) == grp, sel, NEG_INF)
    m1 = jnp.max(masked, axis=0, keepdims=True)
    i1 = jnp.min(jnp.where(masked == m1, row_f, float(N_EXPERTS)), axis=0, keepdims=True)
    rest = jnp.where(row_f == i1, NEG_INF, masked)
    m2 = jnp.max(rest, axis=0, keepdims=True)
    i2 = jnp.min(jnp.where(rest == m2, row_f, float(N_EXPERTS)), axis=0, keepdims=True)
    picked = jnp.where((row_f == i1) | (row_f == i2), scores, 0.0)
    combt_ref[...] = picked / jnp.sum(picked, axis=0, keepdims=True)
    grp_ref[...] = grp


def _out_proj_call(y_ssd, y_da, y_ret, x2d, mod, w_out, ln_g, ln_b, router_w, router_b, seq_len):
    n = x2d.shape[0]
    tm = 512
    per_seq = max(seq_len // tm, 1)
    mod_idx = (lambda i: (i // per_seq, 0, 0)) if mod.shape[0] > 1 else (lambda i: (0, 0, 0))
    row = lambda w: pl.BlockSpec((tm, w), lambda i: (i, 0))
    col = lambda h: pl.BlockSpec((h, tm), lambda i: (0, i))
    full = lambda s: pl.BlockSpec(s, lambda i: (0,) * len(s))
    return pl.pallas_call(
        _out_proj_kernel,
        grid=(n // tm,),
        in_specs=[row(D_SSD), row(D_DA), row(D_RET), row(D_MODEL),
                  pl.BlockSpec((1, 8, D_MODEL), mod_idx),
                  full((D_MODEL, D_MODEL)), full((1, D_MODEL)), full((1, D_MODEL)),
                  full((N_EXPERTS, D_MODEL)), full((N_EXPERTS, 1))],
        out_specs=[row(D_MODEL), row(D_MODEL), col(N_EXPERTS), col(1)],
        out_shape=[jax.ShapeDtypeStruct((n, D_MODEL), F32),
                   jax.ShapeDtypeStruct((n, D_MODEL), BF16),
                   jax.ShapeDtypeStruct((N_EXPERTS, n), F32),
                   jax.ShapeDtypeStruct((1, n), jnp.int32)],
        compiler_params=_params(("arbitrary",)),
        name="out_proj_router",
    )(y_ssd, y_da, y_ret, x2d, mod, w_out, ln_g.reshape(1, D_MODEL), ln_b.reshape(1, D_MODEL),
      router_w.T, router_b.reshape(N_EXPERTS, 1))


MOE_T = 1024
MOE_TILE = 128
MOE_SLOTS = MOE_T + N_GROUPS * MOE_TILE
MOE_COL = 256


def _moe_kernel(xm_ref, combt_ref, grp_ref, wg_ref, wu_ref, wd_ref, x1_ref, mod_ref, g_ref, b_ref, o_ref,
                p_ref, xs_ref, cs_ref, y_ref, seg_ref):
    e = pl.program_id(1)
    n_tiles = MOE_SLOTS // MOE_TILE

    @pl.when(e == 0)
    def _():
        grp = grp_ref[...]
        onehot = lax.broadcasted_iota(jnp.int32, (N_GROUPS, MOE_T), 0) == grp
        onehot_f = jnp.where(onehot, 1.0, 0.0)
        bi = lax.broadcasted_iota(jnp.int32, (128, 128), 0)
        bj = lax.broadcasted_iota(jnp.int32, (128, 128), 1)
        strict_upper = jnp.where(bi < bj, 1.0, 0.0).astype(BF16)
        running = jnp.zeros((N_GROUPS, 1), F32)
        ranks = []
        for blk in range(MOE_T // 128):
            oh_b = onehot_f[:, 128 * blk:128 * (blk + 1)]
            ranks.append(_dot(oh_b.astype(BF16), strict_upper) + running)
            running = running + jnp.sum(oh_b, axis=1, keepdims=True)
        rank = jnp.concatenate(ranks, axis=1)
        tiles = lax.shift_right_logical(running.astype(jnp.int32) + (MOE_TILE - 1), 7)
        starts = [jnp.zeros((1, 1), jnp.int32)]
        for g in range(1, N_GROUPS):
            starts.append(starts[-1] + tiles[g - 1:g, :])
        used = starts[-1] + tiles[N_GROUPS - 1:, :]
        start = jnp.concatenate(starts, axis=0)
        pos = jnp.sum(jnp.where(onehot, (start * MOE_TILE).astype(F32) + rank, 0.0), axis=0, keepdims=True)
        pos = pos.astype(jnp.int32)
        for g in range(N_GROUPS):
            seg_ref[g] = jnp.sum(start[g:g + 1, :])
            seg_ref[N_GROUPS + g] = jnp.sum(tiles[g:g + 1, :])

        def fill(i, c):
            off = pl.multiple_of(i * MOE_TILE, MOE_TILE)
            slot = lax.broadcasted_iota(jnp.int32, (MOE_TILE, MOE_T), 0) + off
            p_ref[pl.ds(off, MOE_TILE), :] = jnp.where(slot == pos, 1.0, 0.0).astype(BF16)
            return c
        lax.fori_loop(0, n_tiles, fill, 0)

        p = p_ref[...]
        for cb in range(D_MODEL // MOE_COL):
            cols = slice(MOE_COL * cb, MOE_COL * (cb + 1))
            xs_ref[:, cols] = _dot(p, xm_ref[:, cols]).astype(BF16)
        ct = combt_ref[...]
        hi = ct.astype(BF16)
        r1 = ct - hi.astype(F32)
        mid = r1.astype(BF16)
        lo = (r1 - mid.astype(F32)).astype(BF16)
        nt = lambda a: lax.dot_general(a, p, (((1,), (1,)), ((), ())), preferred_element_type=F32)
        cs_ref[...] = (nt(hi) + nt(mid) + nt(lo)).T

        def clear(i, c):
            off = pl.multiple_of(i * MOE_TILE, MOE_TILE)
            y_ref[pl.ds(off, MOE_TILE), :] = jnp.zeros((MOE_TILE, D_MODEL), F32)
            return c
        lax.fori_loop(jnp.sum(used), n_tiles, clear, 0)

    g_e = lax.shift_right_logical(e, 2)
    first_tile = seg_ref[g_e]
    group_tiles = seg_ref[N_GROUPS + g_e]
    first_in_group = (e & (EXPERTS_PER_GROUP - 1)) == 0
    lane = lax.broadcasted_iota(jnp.int32, (1, N_EXPERTS), 1)

    def expert_rows(tile, rows):
        off = pl.multiple_of(tile * MOE_TILE, MOE_TILE)
        xs = xs_ref[pl.ds(off, rows), :]
        h = _silu(_dot(xs, wg_ref[0])) * _dot(xs, wu_ref[0])
        y = _dot(h.astype(BF16), wd_ref[0])
        w_e = jnp.sum(jnp.where(lane == e, cs_ref[pl.ds(off, rows), :], 0.0), axis=-1, keepdims=True)

        @pl.when(first_in_group)
        def _():
            y_ref[pl.ds(off, rows), :] = w_e * y

        @pl.when(jnp.logical_not(first_in_group))
        def _():
            y_ref[pl.ds(off, rows), :] += w_e * y

    def two_tiles(i, c):
        expert_rows(first_tile + 2 * i, 2 * MOE_TILE)
        return c
    lax.fori_loop(0, lax.shift_right_logical(group_tiles, 1), two_tiles, 0)

    @pl.when((group_tiles & 1) == 1)
    def _():
        expert_rows(first_tile + group_tiles - 1, MOE_TILE)

    @pl.when(e == N_EXPERTS - 1)
    def _():
        p = p_ref[...]
        for cb in range(D_MODEL // MOE_COL):
            cols = slice(MOE_COL * cb, MOE_COL * (cb + 1))
            o_ref[:, cols] = lax.dot_general(p, y_ref[:, cols].astype(BF16), (((0,), (0,)), ((), ())),
                                             preferred_element_type=F32)
        mod = mod_ref[0]
        o_ref[...] = _layer_norm(ALPHA * x1_ref[...] + mod[5:6] * o_ref[...], g_ref[...], b_ref[...])


def _moe_call(xm2, combt, grp, wg, wu, wd, x1, mod, ln_g, ln_b, seq_len):
    n = xm2.shape[0]
    per_seq = max(seq_len // MOE_T, 1)
    mod_idx = (lambda i, e: (i // per_seq, 0, 0)) if mod.shape[0] > 1 else (lambda i, e: (0, 0, 0))
    row = lambda w: pl.BlockSpec((MOE_T, w), lambda i, e: (i, 0))
    col = lambda h: pl.BlockSpec((h, MOE_T), lambda i, e: (0, i))
    return pl.pallas_call(
        _moe_kernel,
        grid=(n // MOE_T, N_EXPERTS),
        in_specs=[row(D_MODEL), col(N_EXPERTS), col(1),
                  pl.BlockSpec((1, D_MODEL, D_FF), lambda i, e: (e, 0, 0)),
                  pl.BlockSpec((1, D_MODEL, D_FF), lambda i, e: (e, 0, 0)),
                  pl.BlockSpec((1, D_FF, D_MODEL), lambda i, e: (e, 0, 0)),
                  row(D_MODEL),
                  pl.BlockSpec((1, 8, D_MODEL), mod_idx),
                  pl.BlockSpec((1, D_MODEL), lambda i, e: (0, 0)),
                  pl.BlockSpec((1, D_MODEL), lambda i, e: (0, 0))],
        out_specs=row(D_MODEL),
        out_shape=jax.ShapeDtypeStruct((n, D_MODEL), F32),
        scratch_shapes=[pltpu.VMEM((MOE_SLOTS, MOE_T), BF16),
                        pltpu.VMEM((MOE_SLOTS, D_MODEL), BF16),
                        pltpu.VMEM((MOE_SLOTS, N_EXPERTS), F32),
                        pltpu.VMEM((MOE_SLOTS, D_MODEL), F32),
                        pltpu.SMEM((2 * N_GROUPS,), jnp.int32)],
        compiler_params=_params(("arbitrary", "arbitrary")),
        name="moe_grouped",
    )(xm2, combt, grp, wg, wu, wd, x1, mod, ln_g.reshape(1, D_MODEL), ln_b.reshape(1, D_MODEL))


def _rot_weight(w, dim):
    nf, m = dim // 4, dim // 2
    blocks = w.reshape(w.shape[0], -1, dim)
    rot = jnp.concatenate([-blocks[..., nf:m], blocks[..., :nf], -blocks[..., m + nf:], blocks[..., m:m + nf]],
                          axis=-1)
    return rot.reshape(w.shape)


def _pack_in_proj(w_in_l, rope):
    z, xbc, dtw, da, ret = jnp.split(w_in_l, (512, 1280, 1296, 2064), axis=1)
    wq, wk, wv = jnp.split(da, 3, axis=1)
    wrq, wrk, wrv, wrg = jnp.split(ret, 4, axis=1)
    dt_pad = jnp.pad(dtw, ((0, 0), (0, C_QROT - C_DT - dtw.shape[1])))
    nn_parts = [z, xbc, wq, wk, wv, wrq, wrv, wrg, dt_pad]
    nt_parts = [wk.T, wrk.T, jnp.pad(dtw.T, ((0, R_KROT - R_DT - dtw.shape[1]), (0, 0)))]
    if rope:
        nn_parts += [_rot_weight(wq, DA_HALF), _rot_weight(wrq, RET_HEAD_DIM)]
        nt_parts += [_rot_weight(wk, DA_HALF).T, _rot_weight(wrk, RET_HEAD_DIM).T]
    return (jnp.concatenate(nn_parts, axis=1).astype(BF16), jnp.concatenate(nt_parts, axis=0).astype(BF16))


def _rope_tables(seq_len, dim, reps):
    nf = dim // 4
    inv = ROPE_BASE ** (-jnp.arange(nf, dtype=F32) / nf)
    t = jnp.arange(seq_len)
    r = (t // GRID_W).astype(F32)[:, None] * inv
    c = (t % GRID_W).astype(F32)[:, None] * inv
    ang = jnp.tile(jnp.concatenate([r, r, c, c], axis=-1), (1, reps))
    return jnp.cos(ang), jnp.sin(ang)


def _layer(x, mod, l, P, ctx, rope_tabs):
    bsz, seq_len, _ = x.shape
    n = bsz * seq_len
    x2d = x.reshape(n, D_MODEL)
    latent = ctx is not None
    wnn, wnt = P["in_proj"][l][1 if latent else 0]
    outs = _in_proj_call(x2d, mod, wnn, wnt, seq_len, rope_tabs)
    z, xbc, dt, dtt, q, kt, v, rq, rkt, rv, rg = outs[:11]
    r3 = lambda a: a.reshape(bsz, seq_len, a.shape[-1])

    h_ssd0 = ctx[2] if latent else None
    h_ret0 = ctx[3] if latent else None
    ssd_out = _ssd_call(r3(xbc), r3(z), r3(dt), dtt, P["ssd_conv_w"][l], P["ssd_conv_b"][l],
                        P["ssd_dt_bias"][l], P["ssd_a_log"][l], P["ssd_d"][l], P["ssd_norm_w"][l], h_ssd0)
    ret_out = _ret_call(r3(rq), rkt, r3(rv), r3(rg), P["ret_decay"][l], P["ret_norm_w"][l], h_ret0)

    kt_b = jnp.transpose(kt.reshape(D_DA, bsz, seq_len), (1, 0, 2))
    v_b = r3(v)
    if latent:
        y_ssd, y_ret = ssd_out, ret_out
        new = None
        cache_kt = jnp.transpose(ctx[0].reshape(bsz, -1, D_DA), (0, 2, 1)).astype(BF16)
        kt_b = jnp.concatenate([kt_b, cache_kt], axis=2)
        v_b = jnp.concatenate([v_b, ctx[1].reshape(bsz, -1, D_DA)], axis=1)
    else:
        y_ssd, hs = ssd_out
        y_ret, hr = ret_out
        k = outs[11]
        new = (k.reshape(bsz, seq_len, H_DA, DA_HEAD_DIM), v.reshape(bsz, seq_len, H_DA, DA_HEAD_DIM), hs, hr)
    lam_init = 0.8 - 0.6 * math.exp(-0.3 * l)
    y_da = _da_call(r3(q), kt_b, v_b.astype(BF16), P["da_lambda"][l], P["da_norm_w"][l], lam_init)

    x1, xm2, combt, grp = _out_proj_call(y_ssd.reshape(n, D_SSD), y_da.reshape(n, D_DA),
                                         y_ret.reshape(n, D_RET), x2d, mod, P["w_out16"][l],
                                         P["ln_mix_g"][l], P["ln_mix_b"][l], P["router_w"], P["router_b"],
                                         seq_len)
    x2 = _moe_call(xm2, combt, grp, P["wg16"][l], P["wu16"][l], P["wd16"][l], x1, mod,
                   P["ln_ffn_g"][l], P["ln_ffn_b"][l], seq_len)
    return x2.reshape(bsz, seq_len, D_MODEL), new


def kernel(x_prompt, x_sample, c, cache_da_k, cache_da_v, state_ssd, state_ret, c_ctx, w_ada, b_ada, w_in,
           ssd_conv_w, ssd_conv_b, ssd_dt_bias, ssd_a_log, ssd_d, ssd_norm_w, da_lambda, da_norm_w, ret_decay,
           ret_norm_w, w_out, ln_mix_g, ln_mix_b, router_w, router_b, moe_w_gate, moe_w_up, moe_w_down,
           ln_ffn_g, ln_ffn_b):
    dec_b = x_sample.shape[0]
    P = dict(ssd_conv_w=ssd_conv_w, ssd_conv_b=ssd_conv_b, ssd_dt_bias=ssd_dt_bias, ssd_a_log=ssd_a_log,
             ssd_d=ssd_d, ssd_norm_w=ssd_norm_w, da_lambda=da_lambda, da_norm_w=da_norm_w,
             ret_decay=ret_decay, ret_norm_w=ret_norm_w, ln_mix_g=ln_mix_g, ln_mix_b=ln_mix_b,
             router_w=router_w, router_b=router_b, ln_ffn_g=ln_ffn_g, ln_ffn_b=ln_ffn_b)
    P["in_proj"] = [(_pack_in_proj(w_in[l], False), _pack_in_proj(w_in[l], True)) for l in range(DEPTH)]
    P["w_out16"] = w_out.astype(BF16)
    P["wg16"] = moe_w_gate.astype(BF16)
    P["wu16"] = moe_w_up.astype(BF16)
    P["wd16"] = moe_w_down.astype(BF16)

    cond8 = jnp.concatenate([c_ctx[None, :], c, jnp.zeros((8 - 1 - dec_b, D_MODEL), F32)], axis=0)
    mod_all = _ada_call(cond8, w_ada, b_ada).reshape(DEPTH, 8, 6, D_MODEL)
    mod_all = jnp.pad(mod_all, ((0, 0), (0, 0), (0, 2), (0, 0)))

    y = x_prompt
    ks_, vs_, hs_, hr_ = [], [], [], []
    for l in range(DEPTH):
        y, (k_l, v_l, hs_l, hr_l) = _layer(y, mod_all[l, 0:1], l, P, None, None)
        ks_.append(k_l)
        vs_.append(v_l)
        hs_.append(hs_l)
        hr_.append(hr_l)
    new_da_k = jnp.stack(ks_, axis=1)
    new_da_v = jnp.stack(vs_, axis=1)
    new_ssd = jnp.stack(hs_, axis=1)
    new_ret = jnp.stack(hr_, axis=1)

    seq_len = x_sample.shape[1]
    cos_da, sin_da = _rope_tables(seq_len, DA_HALF, D_DA // DA_HALF)
    cos_ret, sin_ret = _rope_tables(seq_len, RET_HEAD_DIM, H_RET)
    rope_tabs = (cos_da, sin_da, cos_ret, sin_ret, cos_da.T, sin_da.T, cos_ret.T, sin_ret.T)
    zl = x_sample
    for l in range(DEPTH):
        ctx = (cache_da_k[:, l], cache_da_v[:, l], state_ssd[:, l], state_ret[:, l])
        zl, _ = _layer(zl, mod_all[l, 1:1 + dec_b], l, P, ctx, rope_tabs)

    return (y, zl, new_da_k, new_da_v, new_ssd, new_ret)
```

```python
import functools
import math

import jax
import jax.numpy as jnp
from jax import lax
from jax.experimental import pallas as pl
from jax.experimental.pallas import tpu as pltpu

D_MODEL = 1024
DEPTH = 2
GRID_W = 64
CHUNK = 128
H_SSD = 8
SSD_HEAD_DIM = 64
D_SSD = H_SSD * SSD_HEAD_DIM
SSD_GROUPS = 2
D_STATE = 64
CONV_W = 5
D_XBC = D_SSD + 2 * SSD_GROUPS * D_STATE
H_DA = 4
DA_HALF = 32
DA_HEAD_DIM = 2 * DA_HALF
D_DA = H_DA * DA_HEAD_DIM
H_RET = 4
RET_HEAD_DIM = 64
D_RET = H_RET * RET_HEAD_DIM
N_EXPERTS = 16
EXPERTS_PER_GROUP = 4
N_GROUPS = N_EXPERTS // EXPERTS_PER_GROUP
D_FF = 512
ROPE_BASE = 10000.0
EPS = 1e-6
ALPHA = (2 * DEPTH) ** 0.25

F32 = jnp.float32
BF16 = jnp.bfloat16
HIGHEST = lax.Precision.HIGHEST
NEG_INF = float("-inf")
LOG2_E = 1.4426950408889634

V7X_VMEM_LIMIT_BYTES = 56 * 1024 * 1024
CONV_HALO = 8

C_Z, C_XBC, C_Q, C_K, C_V, C_RQ, C_RV, C_RG, C_DT, C_QROT, C_RQROT, C_END = (
    0, 512, 1280, 1536, 1792, 2048, 2304, 2560, 2816, 2944, 3200, 3456)
R_K, R_RK, R_DT, R_KROT, R_RKROT, R_END = 0, 256, 512, 544, 800, 1056


def _silu(x):
    return x * (1.0 / (1.0 + jnp.exp(-x)))


def _softplus(x):
    return jnp.maximum(x, 0.0) + jnp.log1p(jnp.exp(-jnp.abs(x)))


def _dot(a, b, precision=None):
    return jnp.dot(a, b, preferred_element_type=F32, precision=precision)


def _layer_norm(t, g, b):
    mu = jnp.mean(t, axis=-1, keepdims=True)
    tc = t - mu
    var = jnp.mean(tc * tc, axis=-1, keepdims=True)
    return tc * lax.rsqrt(var + EPS) * g + b


def _params(sem):
    return pltpu.CompilerParams(dimension_semantics=sem, vmem_limit_bytes=V7X_VMEM_LIMIT_BYTES)


def _ada_kernel(cond_ref, w_ref, b_ref, o_ref):
    o_ref[0] = _dot(_silu(cond_ref[...]), w_ref[0], HIGHEST) + b_ref[0]


def _ada_call(cond8, w_ada, b_ada):
    tn = 1536
    nb = (6 * D_MODEL) // tn
    return pl.pallas_call(
        _ada_kernel,
        grid=(DEPTH, nb),
        in_specs=[pl.BlockSpec((8, D_MODEL), lambda l, n: (0, 0)),
                  pl.BlockSpec((1, D_MODEL, tn), lambda l, n: (l, 0, n)),
                  pl.BlockSpec((1, 1, tn), lambda l, n: (l, 0, n))],
        out_specs=pl.BlockSpec((1, 8, tn), lambda l, n: (l, 0, n)),
        out_shape=jax.ShapeDtypeStruct((DEPTH, 8, 6 * D_MODEL), F32),
        compiler_params=_params(("arbitrary", "arbitrary")),
        name="ada_mod",
    )(cond8, w_ada, b_ada.reshape(DEPTH, 1, 6 * D_MODEL))


def _in_proj_kernel(rope, *refs):
    if rope:
        (x_ref, mod_ref, wnn_ref, wnt_ref, cd_ref, sd_ref, cr_ref, sr_ref,
         cdt_ref, sdt_ref, crt_ref, srt_ref,
         z_ref, xbc_ref, dt_ref, dtt_ref, q_ref, kt_ref, v_ref, rq_ref, rkt_ref, rv_ref, rg_ref) = refs
    else:
        (x_ref, mod_ref, wnn_ref, wnt_ref,
         z_ref, xbc_ref, dt_ref, dtt_ref, q_ref, kt_ref, v_ref, rq_ref, rkt_ref, rv_ref, rg_ref,
         k_ref) = refs
    mod = mod_ref[0]
    xm = (x_ref[...] * (1.0 + mod[1:2]) + mod[0:1]).astype(BF16)

    def nn(a, b):
        return _dot(xm, wnn_ref[:, a:b])

    def nt(a, b):
        return lax.dot_general(wnt_ref[a:b, :], xm, (((1,), (1,)), ((), ())),
                               preferred_element_type=F32)

    z_ref[...] = nn(C_Z, C_XBC)
    xbc_ref[...] = nn(C_XBC, C_Q)
    dt_ref[...] = nn(C_DT, C_QROT)[:, :2 * H_SSD]
    dtt_ref[0] = nt(R_DT, R_KROT)[:2 * H_SSD, :]
    v_ref[...] = nn(C_V, C_RQ).astype(v_ref.dtype)
    rv_ref[...] = nn(C_RV, C_RG)
    rg_ref[...] = nn(C_RG, C_DT)
    q = nn(C_Q, C_K)
    rq = nn(C_RQ, C_RV)
    kt = nt(R_K, R_RK)
    rkt = nt(R_RK, R_DT)
    if rope:
        q = q * cd_ref[...] + nn(C_QROT, C_RQROT) * sd_ref[...]
        rq = rq * cr_ref[...] + nn(C_RQROT, C_END) * sr_ref[...]
        kt = kt * cdt_ref[...] + nt(R_KROT, R_RKROT) * sdt_ref[...]
        rkt = rkt * crt_ref[...] + nt(R_RKROT, R_END) * srt_ref[...]
    else:
        k_ref[...] = nn(C_K, C_V)
    q_ref[...] = (q * (DA_HALF ** -0.5 * LOG2_E)).astype(BF16)
    rq_ref[...] = (rq * (RET_HEAD_DIM ** -0.5)).astype(BF16)
    kt_ref[0] = kt.astype(BF16)
    rkt_ref[0] = rkt.astype(BF16)


def _in_proj_call(x2d, mod, wnn, wnt, seq_len, rope_tabs):
    n = x2d.shape[0]
    tm = 256
    nblk = n // tm
    per_seq = seq_len // tm
    n_mod = mod.shape[0]
    rope = rope_tabs is not None
    mod_idx = (lambda i: (i // per_seq, 0, 0)) if n_mod > 1 else (lambda i: (0, 0, 0))
    bsz = n // seq_len
    row = lambda w: pl.BlockSpec((tm, w), lambda i: (i, 0))
    col = lambda h: pl.BlockSpec((1, h, tm), lambda i: (i // per_seq, 0, i % per_seq))
    in_specs = [row(D_MODEL),
                pl.BlockSpec((1, 8, D_MODEL), mod_idx),
                pl.BlockSpec(wnn.shape, lambda i: (0, 0)),
                pl.BlockSpec(wnt.shape, lambda i: (0, 0))]
    args = [x2d, mod, wnn, wnt]
    if rope:
        in_specs += [pl.BlockSpec((tm, 256), lambda i: (i % per_seq, 0))] * 4
        in_specs += [pl.BlockSpec((256, tm), lambda i: (0, i % per_seq))] * 4
        args += list(rope_tabs)
    sds = jax.ShapeDtypeStruct
    out_shape = [sds((n, D_SSD), F32), sds((n, D_XBC), F32), sds((n, 2 * H_SSD), F32),
                 sds((bsz, 2 * H_SSD, seq_len), F32), sds((n, D_DA), BF16), sds((bsz, D_DA, seq_len), BF16),
                 sds((n, D_DA), BF16 if rope else F32), sds((n, D_RET), BF16), sds((bsz, D_RET, seq_len), BF16),
                 sds((n, D_RET), F32), sds((n, D_RET), F32)]
    out_specs = [row(D_SSD), row(D_XBC), row(2 * H_SSD), col(2 * H_SSD), row(D_DA), col(D_DA),
                 row(D_DA), row(D_RET), col(D_RET), row(D_RET), row(D_RET)]
    if not rope:
        out_shape.append(sds((n, D_DA), F32))
        out_specs.append(row(D_DA))
    return pl.pallas_call(
        functools.partial(_in_proj_kernel, rope),
        grid=(nblk,),
        in_specs=in_specs,
        out_specs=out_specs,
        out_shape=out_shape,
        compiler_params=_params(("arbitrary",)),
        name="in_proj_rope" if rope else "in_proj",
    )(*args)


SEQS_PER_STEP = 4


def _seqs_per_step(bsz):
    assert bsz % SEQS_PER_STEP == 0
    return SEQS_PER_STEP


SSD_SEQ_REFS = (0, 1, 2, 3, 4, 5)
RET_SEQ_REFS = (0, 1, 2, 3)


def _per_sequence(seq_kernel, n_in, seq_in, n_scratch, nc, nb, has_h0, *refs):
    ins = list(refs[:n_in])
    outs = list(refs[n_in:len(refs) - n_scratch])
    scratch = list(refs[len(refs) - n_scratch:])
    for s in range(nb):
        one = lambda r: r.at[pl.ds(s, 1)]
        seq_ins = [one(r) if i in seq_in else r for i, r in enumerate(ins)]
        seq_kernel(nc, has_h0, *seq_ins, *[one(r) for r in outs], *[r.at[s] for r in scratch])


def _ssd_seq(nc, has_h0, *refs):
    refs = list(refs)
    (xm_ref, xp_ref, xn_ref, z_ref, dt_ref, dtt_ref, cw_ref, cb_ref, dtb_c_ref, dtb_r_ref,
     alog_c_ref, alog_r_ref, dskip_ref, nw_ref) = refs[:14]
    refs = refs[14:]
    h0_ref = refs.pop(0) if has_h0 else None
    y_ref = refs.pop(0)
    hs_ref = None if has_h0 else refs.pop(0)
    ext_ref, hp_ref, yf_ref = refs

    d = pl.program_id(1)
    j = pl.program_id(2)
    fwd = d == 0
    cj = jnp.where(fwd, j, nc - 1 - j)

    ext_ref[0:CONV_HALO, :] = jnp.where(cj > 0, xp_ref[0], 0.0)
    ext_ref[CONV_HALO:CONV_HALO + CHUNK, :] = xm_ref[0]
    ext_ref[CONV_HALO + CHUNK:, :] = jnp.where(cj < nc - 1, xn_ref[0], 0.0)
    acc = jnp.zeros((CHUNK, D_XBC), F32) + cb_ref[...]
    for k in range(CONV_W):
        start = CONV_HALO - CONV_W // 2 + k
        acc = acc + cw_ref[k:k + 1, :] * ext_ref[start:start + CHUNK, :]
    act = _silu(acc)
    xs = act[:, :D_SSD]
    bm = act[:, D_SSD:D_SSD + 128]
    cm = act[:, D_SSD + 128:]

    dt_c_all = _softplus(dt_ref[0] + dtb_c_ref[...])
    dt_r_all = _softplus(dtt_ref[0] + dtb_r_ref[...])
    a_c_all = -jnp.exp(alog_c_ref[...])
    a_r_all = -jnp.exp(alog_r_ref[...])
    dt_c = jnp.where(fwd, dt_c_all[:, :H_SSD], dt_c_all[:, H_SSD:])
    dt_r = jnp.where(fwd, dt_r_all[:H_SSD], dt_r_all[H_SSD:])
    la_c = dt_c * jnp.where(fwd, a_c_all[:, :H_SSD], a_c_all[:, H_SSD:])
    la_r = dt_r * jnp.where(fwd, a_r_all[:H_SSD], a_r_all[H_SSD:])

    ii = lax.broadcasted_iota(jnp.int32, (CHUNK, CHUNK), 0)
    jj = lax.broadcasted_iota(jnp.int32, (CHUNK, CHUNK), 1)
    valid = jnp.where(fwd, ii - jj, jj - ii) >= 0
    valid_t = jnp.where(fwd, jj - ii, ii - jj) >= 0
    cum_c = _dot(valid.astype(F32), la_c, HIGHEST)
    cum_r = _dot(la_r, valid_t.astype(F32), HIGHEST)
    tot_r = jnp.where(fwd, cum_r[:, CHUNK - 1:], cum_r[:, :1])
    e_tot = jnp.exp(tot_r)
    wk_r = dt_r * jnp.exp(tot_r - cum_r)
    e_cum_c = jnp.exp(cum_c)

    bm_t = bm.T
    bm_t16 = bm_t.astype(BF16)
    lane = lax.broadcasted_iota(jnp.int32, (1, 128), 1)
    low_lanes = lane < 64
    feat_row = lax.broadcasted_iota(jnp.int32, (128, 1), 0)

    @pl.when(j == 0)
    def _():
        for p in range(H_SSD // 2):
            if has_h0:
                g = p // 2
                both = jnp.concatenate([h0_ref[0, pl.ds(d, 1), 2 * p][0],
                                        h0_ref[0, pl.ds(d, 1), 2 * p + 1][0]], axis=1)
                zero = jnp.zeros((D_STATE, 128), F32)
                hp_ref[p] = jnp.concatenate([both, zero] if g == 0 else [zero, both], axis=0)
            else:
                hp_ref[p] = jnp.zeros((128, 128), F32)

    ys = []
    for g in range(SSD_GROUPS):
        in_group_lane = (lane >= 64 * g) & (lane < 64 * (g + 1))
        cg = jnp.where(in_group_lane, cm, 0.0)
        gram = _dot(cg.astype(BF16), bm_t16)
        in_group_row = (feat_row >= 64 * g) & (feat_row < 64 * (g + 1))
        for p in range(2 * g, 2 * g + 2):
            xs_p = xs[:, 128 * p:128 * (p + 1)].astype(BF16)
            h_prev = hp_ref[p]
            h_prev16 = h_prev.astype(BF16)
            y_h, s_h = [], []
            for hh in (2 * p, 2 * p + 1):
                decay = jnp.exp(jnp.where(valid, cum_c[:, hh:hh + 1] - cum_r[hh:hh + 1, :], NEG_INF))
                sc = gram * decay * dt_r[hh:hh + 1, :]
                y_intra = _dot(sc.astype(BF16), xs_p)
                y_inter = _dot((cg * e_cum_c[:, hh:hh + 1]).astype(BF16), h_prev16)
                y_h.append(y_intra + y_inter)
                s_h.append(_dot((bm_t * wk_r[hh:hh + 1, :]).astype(BF16), xs_p))
            ys.append(jnp.where(low_lanes, y_h[0], y_h[1]))
            s_pair = jnp.where(low_lanes, s_h[0], s_h[1])
            e_pair = jnp.where(low_lanes, e_tot[2 * p:2 * p + 1, :], e_tot[2 * p + 1:2 * p + 2, :])
            hp_ref[p] = jnp.where(in_group_row, e_pair * h_prev + s_pair, 0.0)
    y_dir = jnp.concatenate(ys, axis=1)

    if hs_ref is not None:
        @pl.when(j == nc - 1)
        def _():
            for dd in range(2):
                @pl.when(d == dd)
                def _():
                    for p in range(H_SSD // 2):
                        g = p // 2
                        hn = hp_ref[p]
                        hs_ref[0, dd, 2 * p] = hn[64 * g:64 * (g + 1), :64]
                        hs_ref[0, dd, 2 * p + 1] = hn[64 * g:64 * (g + 1), 64:]

    @pl.when(fwd)
    def _():
        yf_ref[cj] = y_dir.astype(yf_ref.dtype)

    @pl.when(d == 1)
    def _():
        yt = yf_ref[cj].astype(F32) + y_dir + dskip_ref[...] * xs
        gated = yt * _silu(z_ref[0])
        ms = jnp.mean(gated * gated, axis=-1, keepdims=True)
        y_ref[0] = gated * lax.rsqrt(ms + EPS) * nw_ref[...]


def _ssd_call(xbc, z, dt, dtt, conv_w, conv_b, dt_bias, a_log, d_skip, norm_w, h0):
    bsz, seq_len, _ = xbc.shape
    nc = seq_len // CHUNK
    has_h0 = h0 is not None
    hb = CHUNK // CONV_HALO
    last_hb = seq_len // CONV_HALO - 1

    def chunk_of(d, j):
        return jnp.where(d == 0, j, nc - 1 - j)

    def full(shape):
        return pl.BlockSpec(shape, lambda b, d, j: (0,) * len(shape))

    nb = _seqs_per_step(bsz)
    in_specs = [
        pl.BlockSpec((nb, CHUNK, D_XBC), lambda b, d, j: (b, chunk_of(d, j), 0)),
        pl.BlockSpec((nb, CONV_HALO, D_XBC), lambda b, d, j: (b, jnp.maximum(chunk_of(d, j) * hb - 1, 0), 0)),
        pl.BlockSpec((nb, CONV_HALO, D_XBC),
                     lambda b, d, j: (b, jnp.minimum((chunk_of(d, j) + 1) * hb, last_hb), 0)),
        pl.BlockSpec((nb, CHUNK, D_SSD), lambda b, d, j: (b, chunk_of(d, j), 0)),
        pl.BlockSpec((nb, CHUNK, 2 * H_SSD), lambda b, d, j: (b, chunk_of(d, j), 0)),
        pl.BlockSpec((nb, 2 * H_SSD, CHUNK), lambda b, d, j: (b, 0, chunk_of(d, j))),
        full((CONV_W, D_XBC)), full((1, D_XBC)), full((1, 2 * H_SSD)), full((2 * H_SSD, 1)),
        full((1, 2 * H_SSD)), full((2 * H_SSD, 1)), full((1, D_SSD)), full((1, D_SSD)),
    ]
    args = [xbc, xbc, xbc, z, dt, dtt, conv_w, conv_b.reshape(1, D_XBC),
            dt_bias.reshape(1, 2 * H_SSD), dt_bias.reshape(2 * H_SSD, 1),
            a_log.reshape(1, 2 * H_SSD), a_log.reshape(2 * H_SSD, 1),
            jnp.repeat(d_skip, SSD_HEAD_DIM).reshape(1, D_SSD), norm_w.reshape(1, D_SSD)]
    y_spec = pl.BlockSpec((nb, CHUNK, D_SSD), lambda b, d, j: (b, jnp.where(d == 0, nc - 1, nc - 1 - j), 0))
    y_shape = jax.ShapeDtypeStruct((bsz, seq_len, D_SSD), F32)
    state_block = (nb, 2, H_SSD, D_STATE, SSD_HEAD_DIM)
    state_spec = pl.BlockSpec(state_block, lambda b, d, j: (b, 0, 0, 0, 0))
    seq_in = SSD_SEQ_REFS
    if has_h0:
        seq_in = seq_in + (len(in_specs),)
        in_specs.append(state_spec)
        args.append(h0)
        out_specs, out_shape = y_spec, y_shape
    else:
        out_specs = [y_spec, state_spec]
        out_shape = [y_shape, jax.ShapeDtypeStruct((bsz,) + state_block[1:], F32)]
    return pl.pallas_call(
        functools.partial(_per_sequence, _ssd_seq, len(in_specs), seq_in, 3, nc, nb, has_h0),
        grid=(bsz // nb, 2, nc),
        in_specs=in_specs,
        out_specs=out_specs,
        out_shape=out_shape,
        scratch_shapes=[pltpu.VMEM((nb, CHUNK + 2 * CONV_HALO, D_XBC), F32),
                        pltpu.VMEM((nb, H_SSD // 2, 128, 128), F32),
                        pltpu.VMEM((nb, nc, CHUNK, D_SSD), BF16)],
        compiler_params=_params(("arbitrary", "arbitrary", "arbitrary")),
        name="ssd_latent" if has_h0 else "ssd_ctx",
    )(*args)


def _ret_seq(nc, has_h0, *refs):
    refs = list(refs)
    q_ref, kt_ref, v_ref, g_ref, dec_ref, nw_ref = refs[:6]
    refs = refs[6:]
    h0_ref = refs.pop(0) if has_h0 else None
    y_ref = refs.pop(0)
    hs_ref = None if has_h0 else refs.pop(0)
    hr_ref, yf_ref = refs

    d = pl.program_id(1)
    j = pl.program_id(2)
    fwd = d == 0
    cj = jnp.where(fwd, j, nc - 1 - j)

    dec = dec_ref[...]
    lg_all = -_softplus(-dec)
    lg = jnp.where(fwd, lg_all[0:1], lg_all[1:2])

    ii = lax.broadcasted_iota(jnp.int32, (CHUNK, CHUNK), 0)
    jj = lax.broadcasted_iota(jnp.int32, (CHUNK, CHUNK), 1)
    dist = jnp.where(fwd, ii - jj, jj - ii)
    valid = dist >= 0
    dist_f = dist.astype(F32)
    qpos = lax.broadcasted_iota(jnp.int32, (CHUNK, 1), 0)
    kpos = lax.broadcasted_iota(jnp.int32, (1, CHUNK), 1)
    n_q = jnp.where(fwd, qpos + 1, CHUNK - qpos).astype(F32)
    n_k = jnp.where(fwd, CHUNK - 1 - kpos, kpos).astype(F32)
    lane = lax.broadcasted_iota(jnp.int32, (1, 128), 1)
    low_lanes = lane < 64
    low_rows = lax.broadcasted_iota(jnp.int32, (128, 1), 0) < 64
    block_diag = low_rows == low_lanes

    @pl.when(j == 0)
    def _():
        for p in range(H_RET // 2):
            if has_h0:
                zero = jnp.zeros((RET_HEAD_DIM, RET_HEAD_DIM), F32)
                top = jnp.concatenate([h0_ref[0, pl.ds(d, 1), 2 * p][0], zero], axis=1)
                bot = jnp.concatenate([zero, h0_ref[0, pl.ds(d, 1), 2 * p + 1][0]], axis=1)
                hr_ref[p] = jnp.concatenate([top, bot], axis=0)
            else:
                hr_ref[p] = jnp.zeros((128, 128), F32)

    ys = []
    for p in range(H_RET // 2):
        q_p = q_ref[0, :, 128 * p:128 * (p + 1)].astype(F32)
        kt_p = kt_ref[0, 128 * p:128 * (p + 1), :]
        v_p = v_ref[0, :, 128 * p:128 * (p + 1)].astype(BF16)
        h_prev = hr_ref[p]
        lg_a = lg[:, 2 * p:2 * p + 1]
        lg_b = lg[:, 2 * p + 1:2 * p + 2]
        y_h = []
        for lg_h, in_head in ((lg_a, low_lanes), (lg_b, jnp.logical_not(low_lanes))):
            s = _dot(jnp.where(in_head, q_p, 0.0).astype(BF16), kt_p)
            decay = jnp.exp(jnp.where(valid, dist_f * lg_h, NEG_INF))
            y_h.append(_dot((s * decay).astype(BF16), v_p))
        e_q = jnp.where(low_lanes, jnp.exp(n_q * lg_a), jnp.exp(n_q * lg_b))
        y_inter = _dot((q_p * e_q).astype(BF16), h_prev.astype(BF16))
        w_k = jnp.where(low_rows, jnp.exp(n_k * lg_a), jnp.exp(n_k * lg_b))
        s_new = _dot((kt_p.astype(F32) * w_k).astype(BF16), v_p)
        e_tot = jnp.where(low_lanes, jnp.exp(CHUNK * lg_a), jnp.exp(CHUNK * lg_b))
        hr_ref[p] = e_tot * h_prev + jnp.where(block_diag, s_new, 0.0)
        ys.append(jnp.where(low_lanes, y_h[0], y_h[1]) + y_inter)
    y_dir = jnp.concatenate(ys, axis=1)

    if hs_ref is not None:
        @pl.when(j == nc - 1)
        def _():
            for dd in range(2):
                @pl.when(d == dd)
                def _():
                    for p in range(H_RET // 2):
                        hn = hr_ref[p]
                        hs_ref[0, dd, 2 * p] = hn[:64, :64]
                        hs_ref[0, dd, 2 * p + 1] = hn[64:, 64:]

    @pl.when(fwd)
    def _():
        yf_ref[cj] = y_dir.astype(yf_ref.dtype)

    @pl.when(d == 1)
    def _():
        yt = yf_ref[cj].astype(F32) + y_dir
        lane4 = lax.shift_right_logical(lax.broadcasted_iota(jnp.int32, (1, D_RET), 1), 6)
        mu = jnp.zeros_like(yt)
        for h in range(H_RET):
            m_h = jnp.sum(jnp.where(lane4 == h, yt, 0.0), axis=-1, keepdims=True) * (1.0 / RET_HEAD_DIM)
            mu = jnp.where(lane4 == h, m_h, mu)
        yc = yt - mu
        sq = yc * yc
        rs = jnp.zeros_like(yt)
        for h in range(H_RET):
            v_h = jnp.sum(jnp.where(lane4 == h, sq, 0.0), axis=-1, keepdims=True) * (1.0 / RET_HEAD_DIM)
            rs = jnp.where(lane4 == h, lax.rsqrt(v_h + EPS), rs)
        y_ref[0] = _silu(g_ref[0]) * (yc * rs * nw_ref[...])


def _ret_call(rq, rkt, rv, rg, ret_decay, norm_w, h0):
    bsz, seq_len, _ = rq.shape
    nc = seq_len // CHUNK
    has_h0 = h0 is not None

    def chunk_of(d, j):
        return jnp.where(d == 0, j, nc - 1 - j)

    nb = _seqs_per_step(bsz)
    tok = pl.BlockSpec((nb, CHUNK, D_RET), lambda b, d, j: (b, chunk_of(d, j), 0))
    in_specs = [tok,
                pl.BlockSpec((nb, D_RET, CHUNK), lambda b, d, j: (b, 0, chunk_of(d, j))),
                tok, tok,
                pl.BlockSpec((2, H_RET), lambda b, d, j: (0, 0)),
                pl.BlockSpec((1, D_RET), lambda b, d, j: (0, 0))]
    args = [rq, rkt, rv, rg, ret_decay, jnp.tile(norm_w, H_RET).reshape(1, D_RET)]
    y_spec = pl.BlockSpec((nb, CHUNK, D_RET), lambda b, d, j: (b, jnp.where(d == 0, nc - 1, nc - 1 - j), 0))
    y_shape = jax.ShapeDtypeStruct((bsz, seq_len, D_RET), F32)
    state_block = (nb, 2, H_RET, RET_HEAD_DIM, RET_HEAD_DIM)
    state_spec = pl.BlockSpec(state_block, lambda b, d, j: (b, 0, 0, 0, 0))
    seq_in = RET_SEQ_REFS
    if has_h0:
        seq_in = seq_in + (len(in_specs),)
        in_specs.append(state_spec)
        args.append(h0)
        out_specs, out_shape = y_spec, y_shape
    else:
        out_specs = [y_spec, state_spec]
        out_shape = [y_shape, jax.ShapeDtypeStruct((bsz,) + state_block[1:], F32)]
    return pl.pallas_call(
        functools.partial(_per_sequence, _ret_seq, len(in_specs), seq_in, 2, nc, nb, has_h0),
        grid=(bsz // nb, 2, nc),
        in_specs=in_specs,
        out_specs=out_specs,
        out_shape=out_shape,
        scratch_shapes=[pltpu.VMEM((nb, H_RET // 2, 128, 128), F32),
                        pltpu.VMEM((nb, nc, CHUNK, D_RET), BF16)],
        compiler_params=_params(("arbitrary", "arbitrary", "arbitrary")),
        name="ret_latent" if has_h0 else "ret_ctx",
    )(*args)


DA_KEY_CHUNK = 256
DA_ROW_BLOCK = 128


def _da_kernel(lam_init, seg_lens, *refs):
    n_seg = len(seg_lens)
    q_ref = refs[0]
    kt_refs = refs[1:1 + n_seg]
    v_refs = refs[1 + n_seg:1 + 2 * n_seg]
    lamp_ref, nw_ref, o_ref, s_scr, p_scr, va_scr = refs[1 + 2 * n_seg:]
    tq = q_ref.shape[1]
    lk = sum(seg_lens)
    head_of_lane = lax.shift_right_logical(lax.broadcasted_iota(jnp.int32, (1, D_DA), 1), 6)

    @pl.when(pl.program_id(1) == 0)
    def _():
        for h in range(H_DA):
            col = 0
            for seg in range(n_seg):
                v = v_refs[seg][0].astype(F32)
                va_scr[h, col:col + seg_lens[seg], :] = jnp.where(head_of_lane == h, v, 1.0).astype(BF16)
                col += seg_lens[seg]

    lp = lamp_ref[...]
    s1 = jnp.sum(lp[0:1] * lp[1:2], axis=-1, keepdims=True)
    s2 = jnp.sum(lp[2:3] * lp[3:4], axis=-1, keepdims=True)
    lam = jnp.exp(s1) - jnp.exp(s2) + lam_init
    o = jnp.zeros(o_ref.shape[1:], F32)
    for h in range(H_DA):
        pv = []
        for i in range(2):
            off = DA_HEAD_DIM * h + DA_HALF * i
            buf = i
            for rb in range(tq // DA_ROW_BLOCK):
                rows = slice(DA_ROW_BLOCK * rb, DA_ROW_BLOCK * (rb + 1))
                qs = q_ref[0, rows, :][:, off:off + DA_HALF]
                m_run = None
                col = 0
                for seg in range(n_seg):
                    for c in range(seg_lens[seg] // DA_KEY_CHUNK):
                        keys = slice(DA_KEY_CHUNK * c, DA_KEY_CHUNK * (c + 1))
                        s = _dot(qs, kt_refs[seg][0, off:off + DA_HALF, keys])
                        s_scr[buf, rows, col:col + DA_KEY_CHUNK] = s
                        m_run = s if m_run is None else jnp.maximum(m_run, s)
                        col += DA_KEY_CHUNK
                m = jnp.max(m_run, axis=-1, keepdims=True)
                for c in range(lk // DA_KEY_CHUNK):
                    keys = slice(DA_KEY_CHUNK * c, DA_KEY_CHUNK * (c + 1))
                    p_scr[buf, rows, keys] = jnp.exp2(s_scr[buf, rows, keys] - m).astype(BF16)
            pv.append(_dot(p_scr[buf], va_scr[h]))
        foreign = DA_HEAD_DIM * ((h + 1) % H_DA)
        inv0 = 1.0 / pv[0][:, foreign:foreign + 1]
        inv1 = 1.0 / pv[1][:, foreign:foreign + 1]
        o_h = pv[0] * inv0 - pv[1] * (lam * inv1)
        o = jnp.where(head_of_lane == h, o_h, o)
    sq = o * o
    rs = jnp.zeros_like(o)
    for h in range(H_DA):
        ms = jnp.sum(jnp.where(head_of_lane == h, sq, 0.0), axis=-1, keepdims=True) * (1.0 / DA_HEAD_DIM)
        rs = jnp.where(head_of_lane == h, lax.rsqrt(ms + EPS), rs)
    o_ref[0] = o * rs * nw_ref[...] * (1.0 - lam_init)


def _da_call(q, kts, vs, lam_params, norm_w, lam_init):
    bsz, lq, _ = q.shape
    seg_lens = tuple(kt.shape[2] for kt in kts)
    lk = sum(seg_lens)
    tq = 256
    in_specs = [pl.BlockSpec((1, tq, D_DA), lambda b, i: (b, i, 0))]
    in_specs += [pl.BlockSpec((1, D_DA, n), lambda b, i: (b, 0, 0)) for n in seg_lens]
    in_specs += [pl.BlockSpec((1, n, D_DA), lambda b, i: (b, 0, 0)) for n in seg_lens]
    in_specs += [pl.BlockSpec((4, DA_HALF), lambda b, i: (0, 0)),
                 pl.BlockSpec((1, D_DA), lambda b, i: (0, 0))]
    return pl.pallas_call(
        functools.partial(_da_kernel, lam_init, seg_lens),
        grid=(bsz, lq // tq),
        in_specs=in_specs,
        out_specs=pl.BlockSpec((1, tq, D_DA), lambda b, i: (b, i, 0)),
        out_shape=jax.ShapeDtypeStruct((bsz, lq, D_DA), F32),
        scratch_shapes=[pltpu.VMEM((2, tq, lk), F32),
                        pltpu.VMEM((2, tq, lk), BF16),
                        pltpu.VMEM((H_DA, lk, D_DA), BF16)],
        compiler_params=_params(("arbitrary", "arbitrary")),
        name="diff_attn",
    )(q, *kts, *vs, lam_params, jnp.tile(norm_w, H_DA).reshape(1, D_DA))


def _out_proj_kernel(ys_ref, yd_ref, yr_ref, x_ref, mod_ref, w_ref, g_ref, b_ref, rwt_ref, rb_ref,
                     x1_ref, xm2_ref, combt_ref, grp_ref):
    mix = (_dot(ys_ref[...].astype(BF16), w_ref[0:D_SSD, :])
           + _dot(yd_ref[...].astype(BF16), w_ref[D_SSD:D_SSD + D_DA, :])
           + _dot(yr_ref[...].astype(BF16), w_ref[D_SSD + D_DA:, :]))
    mod = mod_ref[0]
    x1 = _layer_norm(ALPHA * x_ref[...] + mod[2:3] * mix, g_ref[...], b_ref[...])
    x1_ref[...] = x1
    xm2 = x1 * (1.0 + mod[4:5]) + mod[3:4]
    xm2_ref[...] = xm2.astype(BF16)

    logits = lax.dot_general(rwt_ref[...], xm2, (((1,), (1,)), ((), ())),
                             preferred_element_type=F32, precision=HIGHEST)
    ex = jnp.exp(logits - jnp.max(logits, axis=0, keepdims=True))
    scores = ex / jnp.sum(ex, axis=0, keepdims=True)
    sel = scores + rb_ref[...]
    row = lax.broadcasted_iota(jnp.int32, sel.shape, 0)
    row_f = row.astype(F32)
    gs = []
    for g in range(N_GROUPS):
        v = [sel[EXPERTS_PER_GROUP * g + i:EXPERTS_PER_GROUP * g + i + 1, :] for i in range(EXPERTS_PER_GROUP)]
        pair_sums = [v[a] + v[b] for a in range(EXPERTS_PER_GROUP) for b in range(a + 1, EXPERTS_PER_GROUP)]
        gs.append(functools.reduce(jnp.maximum, pair_sums))
    best = functools.reduce(jnp.maximum, gs)
    grp = jnp.full(best.shape, N_GROUPS - 1, jnp.int32)
    for g in range(N_GROUPS - 2, -1, -1):
        grp = jnp.where(gs[g] == best, g, grp)
    masked = jnp.where(lax.shift_right_logical(row, 2) == grp, sel, NEG_INF)
    m1 = jnp.max(masked, axis=0, keepdims=True)
    i1 = jnp.min(jnp.where(masked == m1, row_f, float(N_EXPERTS)), axis=0, keepdims=True)
    rest = jnp.where(row_f == i1, NEG_INF, masked)
    m2 = jnp.max(rest, axis=0, keepdims=True)
    i2 = jnp.min(jnp.where(rest == m2, row_f, float(N_EXPERTS)), axis=0, keepdims=True)
    picked = jnp.where((row_f == i1) | (row_f == i2), scores, 0.0)
    combt_ref[...] = picked / jnp.sum(picked, axis=0, keepdims=True)
    grp_ref[...] = grp


def _out_proj_call(y_ssd, y_da, y_ret, x2d, mod, w_out, ln_g, ln_b, router_w, router_b, seq_len):
    n = x2d.shape[0]
    tm = 512
    per_seq = max(seq_len // tm, 1)
    mod_idx = (lambda i: (i // per_seq, 0, 0)) if mod.shape[0] > 1 else (lambda i: (0, 0, 0))
    row = lambda w: pl.BlockSpec((tm, w), lambda i: (i, 0))
    col = lambda h: pl.BlockSpec((h, tm), lambda i: (0, i))
    full = lambda s: pl.BlockSpec(s, lambda i: (0,) * len(s))
    return pl.pallas_call(
        _out_proj_kernel,
        grid=(n // tm,),
        in_specs=[row(D_SSD), row(D_DA), row(D_RET), row(D_MODEL),
                  pl.BlockSpec((1, 8, D_MODEL), mod_idx),
                  full((D_MODEL, D_MODEL)), full((1, D_MODEL)), full((1, D_MODEL)),
                  full((N_EXPERTS, D_MODEL)), full((N_EXPERTS, 1))],
        out_specs=[row(D_MODEL), row(D_MODEL), col(N_EXPERTS), col(1)],
        out_shape=[jax.ShapeDtypeStruct((n, D_MODEL), F32),
                   jax.ShapeDtypeStruct((n, D_MODEL), BF16),
                   jax.ShapeDtypeStruct((N_EXPERTS, n), F32),
                   jax.ShapeDtypeStruct((1, n), jnp.int32)],
        compiler_params=_params(("arbitrary",)),
        name="out_proj_router",
    )(y_ssd, y_da, y_ret, x2d, mod, w_out, ln_g.reshape(1, D_MODEL), ln_b.reshape(1, D_MODEL),
      router_w.T, router_b.reshape(N_EXPERTS, 1))


MOE_T = 1024
MOE_TILE = 128
MOE_SLOTS = MOE_T + N_GROUPS * MOE_TILE
MOE_COL = 256


def _moe_kernel(xm_ref, combt_ref, grp_ref, wg_ref, wu_ref, wd_ref, x1_ref, mod_ref, g_ref, b_ref, o_ref,
                p_ref, xs_ref, cs_ref, y_ref, seg_ref):
    e = pl.program_id(1)
    n_tiles = MOE_SLOTS // MOE_TILE

    @pl.when(e == 0)
    def _():
        grp = grp_ref[...]
        onehot = lax.broadcasted_iota(jnp.int32, (N_GROUPS, MOE_T), 0) == grp
        onehot_f = jnp.where(onehot, 1.0, 0.0)
        bi = lax.broadcasted_iota(jnp.int32, (128, 128), 0)
        bj = lax.broadcasted_iota(jnp.int32, (128, 128), 1)
        strict_upper = jnp.where(bi < bj, 1.0, 0.0).astype(BF16)
        running = jnp.zeros((N_GROUPS, 1), F32)
        ranks = []
        for blk in range(MOE_T // 128):
            oh_b = onehot_f[:, 128 * blk:128 * (blk + 1)]
            ranks.append(_dot(oh_b.astype(BF16), strict_upper) + running)
            running = running + jnp.sum(oh_b, axis=1, keepdims=True)
        rank = jnp.concatenate(ranks, axis=1)
        tiles = lax.shift_right_logical(running.astype(jnp.int32) + (MOE_TILE - 1), 7)
        starts = [jnp.zeros((1, 1), jnp.int32)]
        for g in range(1, N_GROUPS):
            starts.append(starts[-1] + tiles[g - 1:g, :])
        used = starts[-1] + tiles[N_GROUPS - 1:, :]
        start = jnp.concatenate(starts, axis=0)
        pos = jnp.sum(jnp.where(onehot, (start * MOE_TILE).astype(F32) + rank, 0.0), axis=0, keepdims=True)
        pos = pos.astype(jnp.int32)
        for g in range(N_GROUPS):
            seg_ref[g] = jnp.sum(start[g:g + 1, :])
            seg_ref[N_GROUPS + g] = jnp.sum(tiles[g:g + 1, :])

        def fill(i, c):
            off = pl.multiple_of(i * MOE_TILE, MOE_TILE)
            slot = lax.broadcasted_iota(jnp.int32, (MOE_TILE, MOE_T), 0) + off
            p_ref[pl.ds(off, MOE_TILE), :] = jnp.where(slot == pos, 1.0, 0.0).astype(BF16)
            return c
        lax.fori_loop(0, n_tiles, fill, 0)

        p = p_ref[...]
        for cb in range(D_MODEL // MOE_COL):
            cols = slice(MOE_COL * cb, MOE_COL * (cb + 1))
            xs_ref[:, cols] = _dot(p, xm_ref[:, cols]).astype(BF16)
        ct = combt_ref[...]
        hi = ct.astype(BF16)
        r1 = ct - hi.astype(F32)
        mid = r1.astype(BF16)
        lo = (r1 - mid.astype(F32)).astype(BF16)
        nt = lambda a: lax.dot_general(a, p, (((1,), (1,)), ((), ())), preferred_element_type=F32)
        cs_ref[...] = (nt(hi) + nt(mid) + nt(lo)).T

        def clear(i, c):
            off = pl.multiple_of(i * MOE_TILE, MOE_TILE)
            y_ref[pl.ds(off, MOE_TILE), :] = jnp.zeros((MOE_TILE, D_MODEL), F32)
            return c
        lax.fori_loop(jnp.sum(used), n_tiles, clear, 0)

    g_e = lax.shift_right_logical(e, 2)
    first_tile = seg_ref[g_e]
    group_tiles = seg_ref[N_GROUPS + g_e]
    first_in_group = (e & (EXPERTS_PER_GROUP - 1)) == 0
    lane = lax.broadcasted_iota(jnp.int32, (1, N_EXPERTS), 1)

    def expert_rows(tile, rows):
        off = pl.multiple_of(tile * MOE_TILE, MOE_TILE)
        xs = xs_ref[pl.ds(off, rows), :]
        h = _silu(_dot(xs, wg_ref[0])) * _dot(xs, wu_ref[0])
        y = _dot(h.astype(BF16), wd_ref[0])
        w_e = jnp.sum(jnp.where(lane == e, cs_ref[pl.ds(off, rows), :], 0.0), axis=-1, keepdims=True)

        @pl.when(first_in_group)
        def _():
            y_ref[pl.ds(off, rows), :] = w_e * y

        @pl.when(jnp.logical_not(first_in_group))
        def _():
            y_ref[pl.ds(off, rows), :] += w_e * y

    def two_tiles(i, c):
        expert_rows(first_tile + 2 * i, 2 * MOE_TILE)
        return c
    lax.fori_loop(0, lax.shift_right_logical(group_tiles, 1), two_tiles, 0)

    @pl.when((group_tiles & 1) == 1)
    def _():
        expert_rows(first_tile + group_tiles - 1, MOE_TILE)

    @pl.when(e == N_EXPERTS - 1)
    def _():
        p = p_ref[...]
        for cb in range(D_MODEL // MOE_COL):
            cols = slice(MOE_COL * cb, MOE_COL * (cb + 1))
            o_ref[:, cols] = lax.dot_general(p, y_ref[:, cols].astype(BF16), (((0,), (0,)), ((), ())),
                                             preferred_element_type=F32)
        mod = mod_ref[0]
        o_ref[...] = _layer_norm(ALPHA * x1_ref[...] + mod[5:6] * o_ref[...], g_ref[...], b_ref[...])


def _moe_call(xm2, combt, grp, wg, wu, wd, x1, mod, ln_g, ln_b, seq_len):
    n = xm2.shape[0]
    per_seq = max(seq_len // MOE_T, 1)
    mod_idx = (lambda i, e: (i // per_seq, 0, 0)) if mod.shape[0] > 1 else (lambda i, e: (0, 0, 0))
    row = lambda w: pl.BlockSpec((MOE_T, w), lambda i, e: (i, 0))
    col = lambda h: pl.BlockSpec((h, MOE_T), lambda i, e: (0, i))
    return pl.pallas_call(
        _moe_kernel,
        grid=(n // MOE_T, N_EXPERTS),
        in_specs=[row(D_MODEL), col(N_EXPERTS), col(1),
                  pl.BlockSpec((1, D_MODEL, D_FF), lambda i, e: (e, 0, 0)),
                  pl.BlockSpec((1, D_MODEL, D_FF), lambda i, e: (e, 0, 0)),
                  pl.BlockSpec((1, D_FF, D_MODEL), lambda i, e: (e, 0, 0)),
                  row(D_MODEL),
                  pl.BlockSpec((1, 8, D_MODEL), mod_idx),
                  pl.BlockSpec((1, D_MODEL), lambda i, e: (0, 0)),
                  pl.BlockSpec((1, D_MODEL), lambda i, e: (0, 0))],
        out_specs=row(D_MODEL),
        out_shape=jax.ShapeDtypeStruct((n, D_MODEL), F32),
        scratch_shapes=[pltpu.VMEM((MOE_SLOTS, MOE_T), BF16),
                        pltpu.VMEM((MOE_SLOTS, D_MODEL), BF16),
                        pltpu.VMEM((MOE_SLOTS, N_EXPERTS), F32),
                        pltpu.VMEM((MOE_SLOTS, D_MODEL), F32),
                        pltpu.SMEM((2 * N_GROUPS,), jnp.int32)],
        compiler_params=_params(("arbitrary", "arbitrary")),
        name="moe_grouped",
    )(xm2, combt, grp, wg, wu, wd, x1, mod, ln_g.reshape(1, D_MODEL), ln_b.reshape(1, D_MODEL))


def _rot_weight(w, dim):
    nf, m = dim // 4, dim // 2
    blocks = w.reshape(w.shape[0], -1, dim)
    rot = jnp.concatenate([-blocks[..., nf:m], blocks[..., :nf], -blocks[..., m + nf:], blocks[..., m:m + nf]],
                          axis=-1)
    return rot.reshape(w.shape)


def _pack_in_proj(w_in_l, rope):
    z, xbc, dtw, da, ret = jnp.split(w_in_l, (512, 1280, 1296, 2064), axis=1)
    wq, wk, wv = jnp.split(da, 3, axis=1)
    wrq, wrk, wrv, wrg = jnp.split(ret, 4, axis=1)
    dt_pad = jnp.pad(dtw, ((0, 0), (0, C_QROT - C_DT - dtw.shape[1])))
    nn_parts = [z, xbc, wq, wk, wv, wrq, wrv, wrg, dt_pad]
    nt_parts = [wk.T, wrk.T, jnp.pad(dtw.T, ((0, R_KROT - R_DT - dtw.shape[1]), (0, 0)))]
    if rope:
        nn_parts += [_rot_weight(wq, DA_HALF), _rot_weight(wrq, RET_HEAD_DIM)]
        nt_parts += [_rot_weight(wk, DA_HALF).T, _rot_weight(wrk, RET_HEAD_DIM).T]
    return (jnp.concatenate(nn_parts, axis=1).astype(BF16), jnp.concatenate(nt_parts, axis=0).astype(BF16))


def _rope_tables(seq_len, dim, reps):
    nf = dim // 4
    inv = ROPE_BASE ** (-jnp.arange(nf, dtype=F32) / nf)
    t = jnp.arange(seq_len)
    r = (t // GRID_W).astype(F32)[:, None] * inv
    c = (t % GRID_W).astype(F32)[:, None] * inv
    ang = jnp.tile(jnp.concatenate([r, r, c, c], axis=-1), (1, reps))
    return jnp.cos(ang), jnp.sin(ang)


def _layer(x, mod, l, P, ctx, rope_tabs):
    bsz, seq_len, _ = x.shape
    n = bsz * seq_len
    x2d = x.reshape(n, D_MODEL)
    latent = ctx is not None
    wnn, wnt = P["in_proj"][l][1 if latent else 0]
    outs = _in_proj_call(x2d, mod, wnn, wnt, seq_len, rope_tabs)
    z, xbc, dt, dtt, q, kt, v, rq, rkt, rv, rg = outs[:11]
    r3 = lambda a: a.reshape(bsz, seq_len, a.shape[-1])

    h_ssd0 = ctx[2] if latent else None
    h_ret0 = ctx[3] if latent else None
    ssd_out = _ssd_call(r3(xbc), r3(z), r3(dt), dtt, P["ssd_conv_w"][l], P["ssd_conv_b"][l],
                        P["ssd_dt_bias"][l], P["ssd_a_log"][l], P["ssd_d"][l], P["ssd_norm_w"][l], h_ssd0)
    ret_out = _ret_call(r3(rq), rkt, r3(rv), r3(rg), P["ret_decay"][l], P["ret_norm_w"][l], h_ret0)

    kts, vs = [kt], [r3(v)]
    if latent:
        y_ssd, y_ret = ssd_out, ret_out
        new = None
        kts.append(jnp.transpose(ctx[0].reshape(bsz, -1, D_DA), (0, 2, 1)).astype(BF16))
        vs.append(ctx[1].reshape(bsz, -1, D_DA))
    else:
        y_ssd, hs = ssd_out
        y_ret, hr = ret_out
        k = outs[11]
        new = (k.reshape(bsz, seq_len, H_DA, DA_HEAD_DIM), v.reshape(bsz, seq_len, H_DA, DA_HEAD_DIM), hs, hr)
    lam_init = 0.8 - 0.6 * math.exp(-0.3 * l)
    y_da = _da_call(r3(q), kts, vs, P["da_lambda"][l], P["da_norm_w"][l], lam_init)

    x1, xm2, combt, grp = _out_proj_call(y_ssd.reshape(n, D_SSD), y_da.reshape(n, D_DA),
                                         y_ret.reshape(n, D_RET), x2d, mod, P["w_out16"][l],
                                         P["ln_mix_g"][l], P["ln_mix_b"][l], P["router_w"], P["router_b"],
                                         seq_len)
    x2 = _moe_call(xm2, combt, grp, P["wg16"][l], P["wu16"][l], P["wd16"][l], x1, mod,
                   P["ln_ffn_g"][l], P["ln_ffn_b"][l], seq_len)
    return x2.reshape(bsz, seq_len, D_MODEL), new


def kernel(x_prompt, x_sample, c, cache_da_k, cache_da_v, state_ssd, state_ret, c_ctx, w_ada, b_ada, w_in,
           ssd_conv_w, ssd_conv_b, ssd_dt_bias, ssd_a_log, ssd_d, ssd_norm_w, da_lambda, da_norm_w, ret_decay,
           ret_norm_w, w_out, ln_mix_g, ln_mix_b, router_w, router_b, moe_w_gate, moe_w_up, moe_w_down,
           ln_ffn_g, ln_ffn_b):
    dec_b = x_sample.shape[0]
    P = dict(ssd_conv_w=ssd_conv_w, ssd_conv_b=ssd_conv_b, ssd_dt_bias=ssd_dt_bias, ssd_a_log=ssd_a_log,
             ssd_d=ssd_d, ssd_norm_w=ssd_norm_w, da_lambda=da_lambda, da_norm_w=da_norm_w,
             ret_decay=ret_decay, ret_norm_w=ret_norm_w, ln_mix_g=ln_mix_g, ln_mix_b=ln_mix_b,
             router_w=router_w, router_b=router_b, ln_ffn_g=ln_ffn_g, ln_ffn_b=ln_ffn_b)
    P["in_proj"] = [(_pack_in_proj(w_in[l], False), _pack_in_proj(w_in[l], True)) for l in range(DEPTH)]
    P["w_out16"] = w_out.astype(BF16)
    P["wg16"] = moe_w_gate.astype(BF16)
    P["wu16"] = moe_w_up.astype(BF16)
    P["wd16"] = moe_w_down.astype(BF16)

    cond8 = jnp.concatenate([c_ctx[None, :], c, jnp.zeros((8 - 1 - dec_b, D_MODEL), F32)], axis=0)
    mod_all = _ada_call(cond8, w_ada, b_ada).reshape(DEPTH, 8, 6, D_MODEL)
    mod_all = jnp.pad(mod_all, ((0, 0), (0, 0), (0, 2), (0, 0)))

    y = x_prompt
    ks_, vs_, hs_, hr_ = [], [], [], []
    for l in range(DEPTH):
        y, (k_l, v_l, hs_l, hr_l) = _layer(y, mod_all[l, 0:1], l, P, None, None)
        ks_.append(k_l)
        vs_.append(v_l)
        hs_.append(hs_l)
        hr_.append(hr_l)
    new_da_k = jnp.stack(ks_, axis=1)
    new_da_v = jnp.stack(vs_, axis=1)
    new_ssd = jnp.stack(hs_, axis=1)
    new_ret = jnp.stack(hr_, axis=1)

    seq_len = x_sample.shape[1]
    cos_da, sin_da = _rope_tables(seq_len, DA_HALF, D_DA // DA_HALF)
    cos_ret, sin_ret = _rope_tables(seq_len, RET_HEAD_DIM, H_RET)
    rope_tabs = (cos_da, sin_da, cos_ret, sin_ret, cos_da.T, sin_da.T, cos_ret.T, sin_ret.T)
    zl = x_sample
    for l in range(DEPTH):
        ctx = (cache_da_k[:, l], cache_da_v[:, l], state_ssd[:, l], state_ret[:, l])
        zl, _ = _layer(zl, mod_all[l, 1:1 + dec_b], l, P, ctx, rope_tabs)

    return (y, zl, new_da_k, new_da_v, new_ssd, new_ret)
```

```python
import functools
import math

import jax
import jax.numpy as jnp
import numpy as np
from jax import lax
from jax.experimental import pallas as pl
from jax.experimental.pallas import tpu as pltpu

D_MODEL = 1024
DEPTH = 2
GRID_W = 64
CHUNK = 128
H_SSD = 8
SSD_HEAD_DIM = 64
D_SSD = H_SSD * SSD_HEAD_DIM
SSD_GROUPS = 2
D_STATE = 64
CONV_W = 5
D_XBC = D_SSD + 2 * SSD_GROUPS * D_STATE
H_DA = 4
DA_HALF = 32
DA_HEAD_DIM = 2 * DA_HALF
D_DA = H_DA * DA_HEAD_DIM
H_RET = 4
RET_HEAD_DIM = 64
D_RET = H_RET * RET_HEAD_DIM
N_EXPERTS = 16
EXPERTS_PER_GROUP = 4
N_GROUPS = N_EXPERTS // EXPERTS_PER_GROUP
D_FF = 512
ROPE_BASE = 10000.0
EPS = 1e-6
ALPHA = (2 * DEPTH) ** 0.25

F32 = jnp.float32
BF16 = jnp.bfloat16
HIGHEST = lax.Precision.HIGHEST
NEG_INF = float("-inf")
LOG2_E = 1.4426950408889634

V7X_VMEM_LIMIT_BYTES = 56 * 1024 * 1024
CONV_HALO = 8

DT_COLS = 128
DT_ROWS = 32


def _in_proj_layout(rope):
    nn = [("z", D_SSD), ("xbc", D_XBC), ("k", D_DA), ("rq", D_RET), ("rv", D_RET), ("rg", D_RET), ("dt", DT_COLS)]
    nt = [("q", D_DA), ("v", D_DA), ("rk", D_RET), ("dt", DT_ROWS)]
    if rope:
        nn += [("k_rot", D_DA), ("rq_rot", D_RET)]
        nt += [("q_rot", D_DA), ("rk_rot", D_RET)]
    else:
        nn += [("v", D_DA)]

    def spans(parts):
        out, at = {}, 0
        for name, width in parts:
            out[name] = (at, at + width)
            at += width
        return out
    return spans(nn), spans(nt)


def _silu(x):
    return x * (1.0 / (1.0 + jnp.exp(-x)))


def _softplus(x):
    return jnp.maximum(x, 0.0) + jnp.log1p(jnp.exp(-jnp.abs(x)))


def _dot(a, b, precision=None):
    return jnp.dot(a, b, preferred_element_type=F32, precision=precision)


def _layer_norm(t, g, b):
    mu = jnp.mean(t, axis=-1, keepdims=True)
    tc = t - mu
    var = jnp.mean(tc * tc, axis=-1, keepdims=True)
    return tc * lax.rsqrt(var + EPS) * g + b


def _params(sem):
    return pltpu.CompilerParams(dimension_semantics=sem, vmem_limit_bytes=V7X_VMEM_LIMIT_BYTES)


def _ada_kernel(cond_ref, w_ref, b_ref, o_ref):
    o_ref[0] = _dot(_silu(cond_ref[...]), w_ref[0], HIGHEST) + b_ref[0]


def _ada_call(cond8, w_ada, b_ada):
    tn = 1536
    nb = (6 * D_MODEL) // tn
    return pl.pallas_call(
        _ada_kernel,
        grid=(DEPTH, nb),
        in_specs=[pl.BlockSpec((8, D_MODEL), lambda l, n: (0, 0)),
                  pl.BlockSpec((1, D_MODEL, tn), lambda l, n: (l, 0, n)),
                  pl.BlockSpec((1, 1, tn), lambda l, n: (l, 0, n))],
        out_specs=pl.BlockSpec((1, 8, tn), lambda l, n: (l, 0, n)),
        out_shape=jax.ShapeDtypeStruct((DEPTH, 8, 6 * D_MODEL), F32),
        compiler_params=_params(("arbitrary", "arbitrary")),
        name="ada_mod",
    )(cond8, w_ada, b_ada.reshape(DEPTH, 1, 6 * D_MODEL))


def _in_proj_kernel(rope, *refs):
    if rope:
        (x_ref, mod_ref, wnn_ref, wnt_ref, cd_ref, sd_ref, cr_ref, sr_ref,
         cdt_ref, sdt_ref, crt_ref, srt_ref,
         z_ref, xbc_ref, dt_ref, dtt_ref, qt_ref, k_ref, vt_ref, rq_ref, rkt_ref, rv_ref, rg_ref) = refs
    else:
        (x_ref, mod_ref, wnn_ref, wnt_ref,
         z_ref, xbc_ref, dt_ref, dtt_ref, qt_ref, k_ref, vt_ref, rq_ref, rkt_ref, rv_ref, rg_ref,
         v_ref) = refs
    cols, rows = _in_proj_layout(rope)
    mod = mod_ref[0]
    xm = (x_ref[...] * (1.0 + mod[1:2]) + mod[0:1]).astype(BF16)

    def nn(name):
        a, b = cols[name]
        return _dot(xm, wnn_ref[:, a:b])

    def nt(name):
        a, b = rows[name]
        return lax.dot_general(wnt_ref[a:b, :], xm, (((1,), (1,)), ((), ())),
                               preferred_element_type=F32)

    z_ref[...] = nn("z")
    xbc_ref[...] = nn("xbc")
    dt_ref[...] = nn("dt")[:, :2 * H_SSD]
    dtt_ref[0] = nt("dt")[:2 * H_SSD, :]
    vt_ref[0] = nt("v").astype(BF16)
    rv_ref[...] = nn("rv")
    rg_ref[...] = nn("rg")
    qt = nt("q")
    k = nn("k")
    rq = nn("rq")
    rkt = nt("rk")
    if rope:
        qt = qt * cdt_ref[...] + nt("q_rot") * sdt_ref[...]
        k = k * cd_ref[...] + nn("k_rot") * sd_ref[...]
        rq = rq * cr_ref[...] + nn("rq_rot") * sr_ref[...]
        rkt = rkt * crt_ref[...] + nt("rk_rot") * srt_ref[...]
    else:
        v_ref[...] = nn("v")
    qt_ref[0] = (qt * (DA_HALF ** -0.5 * LOG2_E)).astype(BF16)
    k_ref[...] = k.astype(k_ref.dtype)
    rq_ref[...] = (rq * (RET_HEAD_DIM ** -0.5)).astype(BF16)
    rkt_ref[0] = rkt.astype(BF16)


def _in_proj_call(x2d, mod, wnn, wnt, seq_len, rope_tabs):
    n = x2d.shape[0]
    tm = 256
    nblk = n // tm
    per_seq = seq_len // tm
    n_mod = mod.shape[0]
    rope = rope_tabs is not None
    mod_idx = (lambda i: (i // per_seq, 0, 0)) if n_mod > 1 else (lambda i: (0, 0, 0))
    bsz = n // seq_len
    row = lambda w: pl.BlockSpec((tm, w), lambda i: (i, 0))
    col = lambda h: pl.BlockSpec((1, h, tm), lambda i: (i // per_seq, 0, i % per_seq))
    in_specs = [row(D_MODEL),
                pl.BlockSpec((1, 8, D_MODEL), mod_idx),
                pl.BlockSpec(wnn.shape, lambda i: (0, 0)),
                pl.BlockSpec(wnt.shape, lambda i: (0, 0))]
    args = [x2d, mod, wnn, wnt]
    if rope:
        in_specs += [pl.BlockSpec((tm, 256), lambda i: (i % per_seq, 0))] * 4
        in_specs += [pl.BlockSpec((256, tm), lambda i: (0, i % per_seq))] * 4
        args += list(rope_tabs)
    sds = jax.ShapeDtypeStruct
    out_shape = [sds((n, D_SSD), F32), sds((n, D_XBC), F32), sds((n, 2 * H_SSD), F32),
                 sds((bsz, 2 * H_SSD, seq_len), F32), sds((bsz, D_DA, seq_len), BF16),
                 sds((n, D_DA), BF16 if rope else F32), sds((bsz, D_DA, seq_len), BF16),
                 sds((n, D_RET), BF16), sds((bsz, D_RET, seq_len), BF16),
                 sds((n, D_RET), F32), sds((n, D_RET), F32)]
    out_specs = [row(D_SSD), row(D_XBC), row(2 * H_SSD), col(2 * H_SSD), col(D_DA), row(D_DA),
                 col(D_DA), row(D_RET), col(D_RET), row(D_RET), row(D_RET)]
    if not rope:
        out_shape.append(sds((n, D_DA), F32))
        out_specs.append(row(D_DA))
    return pl.pallas_call(
        functools.partial(_in_proj_kernel, rope),
        grid=(nblk,),
        in_specs=in_specs,
        out_specs=out_specs,
        out_shape=out_shape,
        compiler_params=_params(("arbitrary",)),
        name="in_proj_rope" if rope else "in_proj",
    )(*args)


SEQS_PER_STEP = 4


def _seqs_per_step(bsz):
    assert bsz % SEQS_PER_STEP == 0
    return SEQS_PER_STEP


SSD_SEQ_REFS = (0, 1, 2, 3, 4, 5)
RET_SEQ_REFS = (0, 1, 2, 3)


def _per_sequence(seq_kernel, n_in, seq_in, n_scratch, nc, nb, has_h0, *refs):
    ins = list(refs[:n_in])
    outs = list(refs[n_in:len(refs) - n_scratch])
    scratch = list(refs[len(refs) - n_scratch:])
    live = []
    for s in range(nb):
        one = lambda r, s=s: r.at[pl.ds(s, 1)]
        seq_ins = [one(r) if i in seq_in else r for i, r in enumerate(ins)]
        live.append(seq_kernel(nc, has_h0, *seq_ins, *[one(r) for r in outs], *[r.at[s] for r in scratch]))
    while live:
        still = []
        for gen in live:
            try:
                next(gen)
                still.append(gen)
            except StopIteration:
                pass
        live = still


CONV_SHIFT_TAPS = tuple(k for k in range(CONV_W) if k != CONV_W // 2)
CONV_EXT_ROWS = CHUNK + 2 * CONV_HALO
CONV_SHIFT_SHAPE = (len(CONV_SHIFT_TAPS) * CHUNK, 2 * CONV_EXT_ROWS)


def _conv_shift_matrix():
    s = np.zeros(CONV_SHIFT_SHAPE, np.float32)
    t = np.arange(CHUNK)
    for idx, k in enumerate(CONV_SHIFT_TAPS):
        src = CONV_HALO + t + k - CONV_W // 2
        s[idx * CHUNK + t, src] = 1.0
        s[idx * CHUNK + t, CONV_EXT_ROWS + src] = 1.0
    return jnp.asarray(s, dtype=BF16)


def _ssd_seq(nc, has_h0, *refs):
    refs = list(refs)
    (xm_ref, xp_ref, xn_ref, z_ref, dt_ref, dtt_ref, cw_ref, cb_ref, dtb_c_ref, dtb_r_ref,
     alog_c_ref, alog_r_ref, dskip_ref, nw_ref, shift_ref) = refs[:15]
    refs = refs[15:]
    h0_ref = refs.pop(0) if has_h0 else None
    y_ref = refs.pop(0)
    hs_ref = None if has_h0 else refs.pop(0)
    hp_ref, yf_ref = refs

    d = pl.program_id(1)
    j = pl.program_id(2)
    fwd = d == 0
    cj = jnp.where(fwd, j, nc - 1 - j)

    @pl.when(j == 0)
    def _():
        for p in range(H_SSD // 2):
            if has_h0:
                g = p // 2
                both = jnp.concatenate([h0_ref[0, pl.ds(d, 1), 2 * p][0],
                                        h0_ref[0, pl.ds(d, 1), 2 * p + 1][0]], axis=1)
                zero = jnp.zeros((D_STATE, 128), F32)
                hp_ref[p] = jnp.concatenate([both, zero] if g == 0 else [zero, both], axis=0)
            else:
                hp_ref[p] = jnp.zeros((128, 128), F32)
    yield

    x_c = xm_ref[0]
    ext = jnp.concatenate([jnp.where(cj > 0, xp_ref[0], 0.0), x_c, jnp.where(cj < nc - 1, xn_ref[0], 0.0)],
                          axis=0)
    hi = ext.astype(BF16)
    lo = (ext - hi.astype(F32)).astype(BF16)
    shifted = _dot(shift_ref[...], jnp.concatenate([hi, lo], axis=0))
    acc = cb_ref[...] + cw_ref[CONV_W // 2:CONV_W // 2 + 1, :] * x_c
    for idx, k in enumerate(CONV_SHIFT_TAPS):
        acc = acc + cw_ref[k:k + 1, :] * shifted[CHUNK * idx:CHUNK * (idx + 1), :]
    act = _silu(acc)
    xs = act[:, :D_SSD]
    bm = act[:, D_SSD:D_SSD + 128]
    cm = act[:, D_SSD + 128:]
    yield

    dt_c_all = _softplus(dt_ref[0] + dtb_c_ref[...])
    dt_r_all = _softplus(dtt_ref[0] + dtb_r_ref[...])
    a_c_all = -jnp.exp(alog_c_ref[...])
    a_r_all = -jnp.exp(alog_r_ref[...])
    dt_c = jnp.where(fwd, dt_c_all[:, :H_SSD], dt_c_all[:, H_SSD:])
    dt_r = jnp.where(fwd, dt_r_all[:H_SSD], dt_r_all[H_SSD:])
    la_c = dt_c * jnp.where(fwd, a_c_all[:, :H_SSD], a_c_all[:, H_SSD:])
    la_r = dt_r * jnp.where(fwd, a_r_all[:H_SSD], a_r_all[H_SSD:])

    ii = lax.broadcasted_iota(jnp.int32, (CHUNK, CHUNK), 0)
    jj = lax.broadcasted_iota(jnp.int32, (CHUNK, CHUNK), 1)
    valid = jnp.where(fwd, ii - jj, jj - ii) >= 0
    valid_t = jnp.where(fwd, jj - ii, ii - jj) >= 0
    cum_c = _dot(valid.astype(F32), la_c, HIGHEST)
    cum_r = _dot(la_r, valid_t.astype(F32), HIGHEST)
    tot_r = jnp.where(fwd, cum_r[:, CHUNK - 1:], cum_r[:, :1])
    e_tot = jnp.exp(tot_r)
    wk_r = dt_r * jnp.exp(tot_r - cum_r)
    e_cum_c = jnp.exp(cum_c)

    yield
    bm_t = bm.T
    bm_t16 = bm_t.astype(BF16)
    lane = lax.broadcasted_iota(jnp.int32, (1, 128), 1)
    low_lanes = lane < 64
    feat_row = lax.broadcasted_iota(jnp.int32, (128, 1), 0)

    ys = []
    for g in range(SSD_GROUPS):
        in_group_lane = (lane >= 64 * g) & (lane < 64 * (g + 1))
        cg = jnp.where(in_group_lane, cm, 0.0)
        gram = _dot(cg.astype(BF16), bm_t16)
        in_group_row = (feat_row >= 64 * g) & (feat_row < 64 * (g + 1))
        for p in range(2 * g, 2 * g + 2):
            xs_p = xs[:, 128 * p:128 * (p + 1)].astype(BF16)
            h_prev = hp_ref[p]
            h_prev16 = h_prev.astype(BF16)
            y_h, s_h = [], []
            for hh in (2 * p, 2 * p + 1):
                decay = jnp.exp(jnp.where(valid, cum_c[:, hh:hh + 1] - cum_r[hh:hh + 1, :], NEG_INF))
                sc = gram * decay * dt_r[hh:hh + 1, :]
                y_intra = _dot(sc.astype(BF16), xs_p)
                y_inter = _dot((cg * e_cum_c[:, hh:hh + 1]).astype(BF16), h_prev16)
                y_h.append(y_intra + y_inter)
                s_h.append(_dot((bm_t * wk_r[hh:hh + 1, :]).astype(BF16), xs_p))
            ys.append(jnp.where(low_lanes, y_h[0], y_h[1]))
            s_pair = jnp.where(low_lanes, s_h[0], s_h[1])
            e_pair = jnp.where(low_lanes, e_tot[2 * p:2 * p + 1, :], e_tot[2 * p + 1:2 * p + 2, :])
            hp_ref[p] = jnp.where(in_group_row, e_pair * h_prev + s_pair, 0.0)
            yield
    y_dir = jnp.concatenate(ys, axis=1)

    if hs_ref is not None:
        @pl.when(j == nc - 1)
        def _():
            for dd in range(2):
                @pl.when(d == dd)
                def _():
                    for p in range(H_SSD // 2):
                        g = p // 2
                        hn = hp_ref[p]
                        hs_ref[0, dd, 2 * p] = hn[64 * g:64 * (g + 1), :64]
                        hs_ref[0, dd, 2 * p + 1] = hn[64 * g:64 * (g + 1), 64:]

    @pl.when(fwd)
    def _():
        yf_ref[cj] = y_dir.astype(yf_ref.dtype)

    @pl.when(d == 1)
    def _():
        yt = yf_ref[cj].astype(F32) + y_dir + dskip_ref[...] * xs
        gated = yt * _silu(z_ref[0])
        ms = jnp.mean(gated * gated, axis=-1, keepdims=True)
        y_ref[0] = gated * lax.rsqrt(ms + EPS) * nw_ref[...]


def _ssd_call(xbc, z, dt, dtt, conv_w, conv_b, dt_bias, a_log, d_skip, norm_w, h0):
    bsz, seq_len, _ = xbc.shape
    nc = seq_len // CHUNK
    has_h0 = h0 is not None
    hb = CHUNK // CONV_HALO
    last_hb = seq_len // CONV_HALO - 1

    def chunk_of(d, j):
        return jnp.where(d == 0, j, nc - 1 - j)

    def full(shape):
        return pl.BlockSpec(shape, lambda b, d, j: (0,) * len(shape))

    nb = _seqs_per_step(bsz)
    in_specs = [
        pl.BlockSpec((nb, CHUNK, D_XBC), lambda b, d, j: (b, chunk_of(d, j), 0)),
        pl.BlockSpec((nb, CONV_HALO, D_XBC), lambda b, d, j: (b, jnp.maximum(chunk_of(d, j) * hb - 1, 0), 0)),
        pl.BlockSpec((nb, CONV_HALO, D_XBC),
                     lambda b, d, j: (b, jnp.minimum((chunk_of(d, j) + 1) * hb, last_hb), 0)),
        pl.BlockSpec((nb, CHUNK, D_SSD), lambda b, d, j: (b, chunk_of(d, j), 0)),
        pl.BlockSpec((nb, CHUNK, 2 * H_SSD), lambda b, d, j: (b, chunk_of(d, j), 0)),
        pl.BlockSpec((nb, 2 * H_SSD, CHUNK), lambda b, d, j: (b, 0, chunk_of(d, j))),
        full((CONV_W, D_XBC)), full((1, D_XBC)), full((1, 2 * H_SSD)), full((2 * H_SSD, 1)),
        full((1, 2 * H_SSD)), full((2 * H_SSD, 1)), full((1, D_SSD)), full((1, D_SSD)),
        full(CONV_SHIFT_SHAPE),
    ]
    args = [xbc, xbc, xbc, z, dt, dtt, conv_w, conv_b.reshape(1, D_XBC),
            dt_bias.reshape(1, 2 * H_SSD), dt_bias.reshape(2 * H_SSD, 1),
            a_log.reshape(1, 2 * H_SSD), a_log.reshape(2 * H_SSD, 1),
            jnp.repeat(d_skip, SSD_HEAD_DIM).reshape(1, D_SSD), norm_w.reshape(1, D_SSD),
            _conv_shift_matrix()]
    y_spec = pl.BlockSpec((nb, CHUNK, D_SSD), lambda b, d, j: (b, jnp.where(d == 0, nc - 1, nc - 1 - j), 0))
    y_shape = jax.ShapeDtypeStruct((bsz, seq_len, D_SSD), F32)
    state_block = (nb, 2, H_SSD, D_STATE, SSD_HEAD_DIM)
    state_spec = pl.BlockSpec(state_block, lambda b, d, j: (b, 0, 0, 0, 0))
    seq_in = SSD_SEQ_REFS
    if has_h0:
        seq_in = seq_in + (len(in_specs),)
        in_specs.append(state_spec)
        args.append(h0)
        out_specs, out_shape = y_spec, y_shape
    else:
        out_specs = [y_spec, state_spec]
        out_shape = [y_shape, jax.ShapeDtypeStruct((bsz,) + state_block[1:], F32)]
    return pl.pallas_call(
        functools.partial(_per_sequence, _ssd_seq, len(in_specs), seq_in, 2, nc, nb, has_h0),
        grid=(bsz // nb, 2, nc),
        in_specs=in_specs,
        out_specs=out_specs,
        out_shape=out_shape,
        scratch_shapes=[pltpu.VMEM((nb, H_SSD // 2, 128, 128), F32),
                        pltpu.VMEM((nb, nc, CHUNK, D_SSD), BF16)],
        compiler_params=_params(("arbitrary", "arbitrary", "arbitrary")),
        name="ssd_latent" if has_h0 else "ssd_ctx",
    )(*args)


def _ret_seq(nc, has_h0, *refs):
    refs = list(refs)
    q_ref, kt_ref, v_ref, g_ref, dec_ref, nw_ref = refs[:6]
    refs = refs[6:]
    h0_ref = refs.pop(0) if has_h0 else None
    y_ref = refs.pop(0)
    hs_ref = None if has_h0 else refs.pop(0)
    hr_ref, yf_ref = refs

    d = pl.program_id(1)
    j = pl.program_id(2)
    fwd = d == 0
    cj = jnp.where(fwd, j, nc - 1 - j)

    @pl.when(j == 0)
    def _():
        for p in range(H_RET // 2):
            if has_h0:
                zero = jnp.zeros((RET_HEAD_DIM, RET_HEAD_DIM), F32)
                top = jnp.concatenate([h0_ref[0, pl.ds(d, 1), 2 * p][0], zero], axis=1)
                bot = jnp.concatenate([zero, h0_ref[0, pl.ds(d, 1), 2 * p + 1][0]], axis=1)
                hr_ref[p] = jnp.concatenate([top, bot], axis=0)
            else:
                hr_ref[p] = jnp.zeros((128, 128), F32)
    yield

    dec = dec_ref[...]
    lg_all = -_softplus(-dec)
    lg = jnp.where(fwd, lg_all[0:1], lg_all[1:2])

    ii = lax.broadcasted_iota(jnp.int32, (CHUNK, CHUNK), 0)
    jj = lax.broadcasted_iota(jnp.int32, (CHUNK, CHUNK), 1)
    dist = jnp.where(fwd, ii - jj, jj - ii)
    valid = dist >= 0
    dist_f = dist.astype(F32)
    qpos = lax.broadcasted_iota(jnp.int32, (CHUNK, 1), 0)
    kpos = lax.broadcasted_iota(jnp.int32, (1, CHUNK), 1)
    n_q = jnp.where(fwd, qpos + 1, CHUNK - qpos).astype(F32)
    n_k = jnp.where(fwd, CHUNK - 1 - kpos, kpos).astype(F32)
    lane = lax.broadcasted_iota(jnp.int32, (1, 128), 1)
    low_lanes = lane < 64
    low_rows = lax.broadcasted_iota(jnp.int32, (128, 1), 0) < 64
    block_diag = low_rows == low_lanes

    ys = []
    for p in range(H_RET // 2):
        q_p = q_ref[0, :, 128 * p:128 * (p + 1)].astype(F32)
        kt_p = kt_ref[0, 128 * p:128 * (p + 1), :]
        v_p = v_ref[0, :, 128 * p:128 * (p + 1)].astype(BF16)
        h_prev = hr_ref[p]
        lg_a = lg[:, 2 * p:2 * p + 1]
        lg_b = lg[:, 2 * p + 1:2 * p + 2]
        y_h = []
        for lg_h, in_head in ((lg_a, low_lanes), (lg_b, jnp.logical_not(low_lanes))):
            s = _dot(jnp.where(in_head, q_p, 0.0).astype(BF16), kt_p)
            decay = jnp.exp(jnp.where(valid, dist_f * lg_h, NEG_INF))
            y_h.append(_dot((s * decay).astype(BF16), v_p))
        e_q = jnp.where(low_lanes, jnp.exp(n_q * lg_a), jnp.exp(n_q * lg_b))
        y_inter = _dot((q_p * e_q).astype(BF16), h_prev.astype(BF16))
        w_k = jnp.where(low_rows, jnp.exp(n_k * lg_a), jnp.exp(n_k * lg_b))
        s_new = _dot((kt_p.astype(F32) * w_k).astype(BF16), v_p)
        e_tot = jnp.where(low_lanes, jnp.exp(CHUNK * lg_a), jnp.exp(CHUNK * lg_b))
        hr_ref[p] = e_tot * h_prev + jnp.where(block_diag, s_new, 0.0)
        ys.append(jnp.where(low_lanes, y_h[0], y_h[1]) + y_inter)
        yield
    y_dir = jnp.concatenate(ys, axis=1)

    if hs_ref is not None:
        @pl.when(j == nc - 1)
        def _():
            for dd in range(2):
                @pl.when(d == dd)
                def _():
                    for p in range(H_RET // 2):
                        hn = hr_ref[p]
                        hs_ref[0, dd, 2 * p] = hn[:64, :64]
                        hs_ref[0, dd, 2 * p + 1] = hn[64:, 64:]

    @pl.when(fwd)
    def _():
        yf_ref[cj] = y_dir.astype(yf_ref.dtype)

    @pl.when(d == 1)
    def _():
        yt = yf_ref[cj].astype(F32) + y_dir
        lane4 = lax.shift_right_logical(lax.broadcasted_iota(jnp.int32, (1, D_RET), 1), 6)
        mu = jnp.zeros_like(yt)
        for h in range(H_RET):
            m_h = jnp.sum(jnp.where(lane4 == h, yt, 0.0), axis=-1, keepdims=True) * (1.0 / RET_HEAD_DIM)
            mu = jnp.where(lane4 == h, m_h, mu)
        yc = yt - mu
        sq = yc * yc
        rs = jnp.zeros_like(yt)
        for h in range(H_RET):
            v_h = jnp.sum(jnp.where(lane4 == h, sq, 0.0), axis=-1, keepdims=True) * (1.0 / RET_HEAD_DIM)
            rs = jnp.where(lane4 == h, lax.rsqrt(v_h + EPS), rs)
        y_ref[0] = _silu(g_ref[0]) * (yc * rs * nw_ref[...])


def _ret_call(rq, rkt, rv, rg, ret_decay, norm_w, h0):
    bsz, seq_len, _ = rq.shape
    nc = seq_len // CHUNK
    has_h0 = h0 is not None

    def chunk_of(d, j):
        return jnp.where(d == 0, j, nc - 1 - j)

    nb = _seqs_per_step(bsz)
    tok = pl.BlockSpec((nb, CHUNK, D_RET), lambda b, d, j: (b, chunk_of(d, j), 0))
    in_specs = [tok,
                pl.BlockSpec((nb, D_RET, CHUNK), lambda b, d, j: (b, 0, chunk_of(d, j))),
                tok, tok,
                pl.BlockSpec((2, H_RET), lambda b, d, j: (0, 0)),
                pl.BlockSpec((1, D_RET), lambda b, d, j: (0, 0))]
    args = [rq, rkt, rv, rg, ret_decay, jnp.tile(norm_w, H_RET).reshape(1, D_RET)]
    y_spec = pl.BlockSpec((nb, CHUNK, D_RET), lambda b, d, j: (b, jnp.where(d == 0, nc - 1, nc - 1 - j), 0))
    y_shape = jax.ShapeDtypeStruct((bsz, seq_len, D_RET), F32)
    state_block = (nb, 2, H_RET, RET_HEAD_DIM, RET_HEAD_DIM)
    state_spec = pl.BlockSpec(state_block, lambda b, d, j: (b, 0, 0, 0, 0))
    seq_in = RET_SEQ_REFS
    if has_h0:
        seq_in = seq_in + (len(in_specs),)
        in_specs.append(state_spec)
        args.append(h0)
        out_specs, out_shape = y_spec, y_shape
    else:
        out_specs = [y_spec, state_spec]
        out_shape = [y_shape, jax.ShapeDtypeStruct((bsz,) + state_block[1:], F32)]
    return pl.pallas_call(
        functools.partial(_per_sequence, _ret_seq, len(in_specs), seq_in, 2, nc, nb, has_h0),
        grid=(bsz // nb, 2, nc),
        in_specs=in_specs,
        out_specs=out_specs,
        out_shape=out_shape,
        scratch_shapes=[pltpu.VMEM((nb, H_RET // 2, 128, 128), F32),
                        pltpu.VMEM((nb, nc, CHUNK, D_RET), BF16)],
        compiler_params=_params(("arbitrary", "arbitrary", "arbitrary")),
        name="ret_latent" if has_h0 else "ret_ctx",
    )(*args)


DA_KEY_CHUNK = 256
DA_SUM_ROWS = 16


def _da_kernel(lam_init, seg_lens, *refs):
    n_seg = len(seg_lens)
    qt_ref = refs[0]
    k_refs = refs[1:1 + n_seg]
    vt_refs = refs[1 + n_seg:1 + 2 * n_seg]
    lamp_ref, nw_ref, o_ref, s_scr, p_scr, va_scr = refs[1 + 2 * n_seg:]
    tq = qt_ref.shape[2]
    lk = sum(seg_lens)
    va_rows = DA_HEAD_DIM + DA_SUM_ROWS

    @pl.when(pl.program_id(1) == 0)
    def _():
        for h in range(H_DA):
            col = 0
            for seg in range(n_seg):
                keys = slice(col, col + seg_lens[seg])
                va_scr[h, 0:DA_HEAD_DIM, keys] = vt_refs[seg][0, DA_HEAD_DIM * h:DA_HEAD_DIM * (h + 1), :]
                va_scr[h, DA_HEAD_DIM:va_rows, keys] = jnp.ones((DA_SUM_ROWS, seg_lens[seg]), BF16)
                col += seg_lens[seg]

    lp = lamp_ref[...]
    s1 = jnp.sum(lp[0:1] * lp[1:2], axis=-1, keepdims=True)
    s2 = jnp.sum(lp[2:3] * lp[3:4], axis=-1, keepdims=True)
    lam = jnp.exp(s1) - jnp.exp(s2) + lam_init
    qt = qt_ref[0]
    half_of_row = lax.shift_right_logical(lax.broadcasted_iota(jnp.int32, (D_DA, 1), 0), 5)
    chunks = [(seg, DA_KEY_CHUNK * c) for seg in range(n_seg) for c in range(seg_lens[seg] // DA_KEY_CHUNK)]
    n_soft = 2 * H_DA
    q_half, m8 = {}, {}

    def scores(j, ci):
        seg, start = chunks[ci]
        if j not in q_half:
            q_half[j] = jnp.where(half_of_row == j, qt, jnp.zeros_like(qt))
        k_c = k_refs[seg][0, start:start + DA_KEY_CHUNK, :].astype(BF16)
        s = _dot(k_c, q_half[j])
        s_scr[j % 2, DA_KEY_CHUNK * ci:DA_KEY_CHUNK * (ci + 1), :] = s
        for r in range(DA_KEY_CHUNK // 8):
            part = s[8 * r:8 * (r + 1), :]
            m8[j] = part if j not in m8 else jnp.maximum(m8[j], part)

    for ci in range(len(chunks)):
        scores(0, ci)
    o_heads, pv = [], []
    for j in range(n_soft):
        m = jnp.max(m8[j], axis=0, keepdims=True)
        for ci in range(len(chunks)):
            if j + 1 < n_soft:
                scores(j + 1, ci)
            keys = slice(DA_KEY_CHUNK * ci, DA_KEY_CHUNK * (ci + 1))
            p_scr[j % 2, keys, :] = jnp.exp2(s_scr[j % 2, keys, :] - m).astype(BF16)
        pv.append(_dot(va_scr[j // 2], p_scr[j % 2]))
        if j % 2 == 1:
            inv0 = 1.0 / pv[0][DA_HEAD_DIM:DA_HEAD_DIM + 1, :]
            inv1 = 1.0 / pv[1][DA_HEAD_DIM:DA_HEAD_DIM + 1, :]
            o_h = pv[0][:DA_HEAD_DIM, :] * inv0 - pv[1][:DA_HEAD_DIM, :] * (lam * inv1)
            ms = jnp.mean(o_h * o_h, axis=0, keepdims=True)
            o_heads.append(o_h * lax.rsqrt(ms + EPS) * nw_ref[...] * (1.0 - lam_init))
            pv = []
    o_ref[0] = jnp.concatenate(o_heads, axis=0).T


def _da_call(qt, ks, vts, lam_params, norm_w, lam_init):
    bsz, _, lq = qt.shape
    seg_lens = tuple(k.shape[1] for k in ks)
    lk = sum(seg_lens)
    tq = 256
    in_specs = [pl.BlockSpec((1, D_DA, tq), lambda b, i: (b, 0, i))]
    in_specs += [pl.BlockSpec((1, n, D_DA), lambda b, i: (b, 0, 0)) for n in seg_lens]
    in_specs += [pl.BlockSpec((1, D_DA, n), lambda b, i: (b, 0, 0)) for n in seg_lens]
    in_specs += [pl.BlockSpec((4, DA_HALF), lambda b, i: (0, 0)),
                 pl.BlockSpec((DA_HEAD_DIM, 1), lambda b, i: (0, 0))]
    return pl.pallas_call(
        functools.partial(_da_kernel, lam_init, seg_lens),
        grid=(bsz, lq // tq),
        in_specs=in_specs,
        out_specs=pl.BlockSpec((1, tq, D_DA), lambda b, i: (b, i, 0)),
        out_shape=jax.ShapeDtypeStruct((bsz, lq, D_DA), F32),
        scratch_shapes=[pltpu.VMEM((2, lk, tq), F32),
                        pltpu.VMEM((2, lk, tq), BF16),
                        pltpu.VMEM((H_DA, DA_HEAD_DIM + DA_SUM_ROWS, lk), BF16)],
        compiler_params=_params(("arbitrary", "arbitrary")),
        name="diff_attn",
    )(qt, *ks, *vts, lam_params, norm_w.reshape(DA_HEAD_DIM, 1))


def _out_proj_kernel(ys_ref, yd_ref, yr_ref, x_ref, mod_ref, w_ref, g_ref, b_ref, rwt_ref, rb_ref,
                     x1_ref, xm2_ref, combt_ref, grp_ref):
    mix = (_dot(ys_ref[...].astype(BF16), w_ref[0:D_SSD, :])
           + _dot(yd_ref[...].astype(BF16), w_ref[D_SSD:D_SSD + D_DA, :])
           + _dot(yr_ref[...].astype(BF16), w_ref[D_SSD + D_DA:, :]))
    mod = mod_ref[0]
    x1 = _layer_norm(ALPHA * x_ref[...] + mod[2:3] * mix, g_ref[...], b_ref[...])
    x1_ref[...] = x1
    xm2 = x1 * (1.0 + mod[4:5]) + mod[3:4]
    xm2_ref[...] = xm2.astype(BF16)

    logits = lax.dot_general(rwt_ref[...], xm2, (((1,), (1,)), ((), ())),
                             preferred_element_type=F32, precision=HIGHEST)
    ex = jnp.exp(logits - jnp.max(logits, axis=0, keepdims=True))
    scores = ex / jnp.sum(ex, axis=0, keepdims=True)
    sel = scores + rb_ref[...]
    row = lax.broadcasted_iota(jnp.int32, sel.shape, 0)
    row_f = row.astype(F32)
    gs = []
    for g in range(N_GROUPS):
        v = [sel[EXPERTS_PER_GROUP * g + i:EXPERTS_PER_GROUP * g + i + 1, :] for i in range(EXPERTS_PER_GROUP)]
        pair_sums = [v[a] + v[b] for a in range(EXPERTS_PER_GROUP) for b in range(a + 1, EXPERTS_PER_GROUP)]
        gs.append(functools.reduce(jnp.maximum, pair_sums))
    best = functools.reduce(jnp.maximum, gs)
    grp = jnp.full(best.shape, N_GROUPS - 1, jnp.int32)
    for g in range(N_GROUPS - 2, -1, -1):
        grp = jnp.where(gs[g] == best, g, grp)
    masked = jnp.where(lax.shift_right_logical(row, 2) == grp, sel, NEG_INF)
    m1 = jnp.max(masked, axis=0, keepdims=True)
    i1 = jnp.min(jnp.where(masked == m1, row_f, float(N_EXPERTS)), axis=0, keepdims=True)
    rest = jnp.where(row_f == i1, NEG_INF, masked)
    m2 = jnp.max(rest, axis=0, keepdims=True)
    i2 = jnp.min(jnp.where(rest == m2, row_f, float(N_EXPERTS)), axis=0, keepdims=True)
    picked = jnp.where((row_f == i1) | (row_f == i2), scores, 0.0)
    combt_ref[...] = picked / jnp.sum(picked, axis=0, keepdims=True)
    grp_ref[...] = grp


def _out_proj_call(y_ssd, y_da, y_ret, x2d, mod, w_out, ln_g, ln_b, router_w, router_b, seq_len):
    n = x2d.shape[0]
    tm = 512
    per_seq = max(seq_len // tm, 1)
    mod_idx = (lambda i: (i // per_seq, 0, 0)) if mod.shape[0] > 1 else (lambda i: (0, 0, 0))
    row = lambda w: pl.BlockSpec((tm, w), lambda i: (i, 0))
    col = lambda h: pl.BlockSpec((h, tm), lambda i: (0, i))
    full = lambda s: pl.BlockSpec(s, lambda i: (0,) * len(s))
    return pl.pallas_call(
        _out_proj_kernel,
        grid=(n // tm,),
        in_specs=[row(D_SSD), row(D_DA), row(D_RET), row(D_MODEL),
                  pl.BlockSpec((1, 8, D_MODEL), mod_idx),
                  full((D_MODEL, D_MODEL)), full((1, D_MODEL)), full((1, D_MODEL)),
                  full((N_EXPERTS, D_MODEL)), full((N_EXPERTS, 1))],
        out_specs=[row(D_MODEL), row(D_MODEL), col(N_EXPERTS), col(1)],
        out_shape=[jax.ShapeDtypeStruct((n, D_MODEL), F32),
                   jax.ShapeDtypeStruct((n, D_MODEL), BF16),
                   jax.ShapeDtypeStruct((N_EXPERTS, n), F32),
                   jax.ShapeDtypeStruct((1, n), jnp.int32)],
        compiler_params=_params(("arbitrary",)),
        name="out_proj_router",
    )(y_ssd, y_da, y_ret, x2d, mod, w_out, ln_g.reshape(1, D_MODEL), ln_b.reshape(1, D_MODEL),
      router_w.T, router_b.reshape(N_EXPERTS, 1))


MOE_T = 1024
MOE_TILE = 128
MOE_SLOTS = MOE_T + N_GROUPS * MOE_TILE
MOE_COL = 256


def _moe_kernel(xm_ref, combt_ref, grp_ref, wg_ref, wu_ref, wd_ref, x1_ref, mod_ref, g_ref, b_ref, o_ref,
                p_ref, xs_ref, cs_ref, y_ref, seg_ref):
    e = pl.program_id(1)
    n_tiles = MOE_SLOTS // MOE_TILE

    @pl.when(e == 0)
    def _():
        grp = grp_ref[...]
        onehot = lax.broadcasted_iota(jnp.int32, (N_GROUPS, MOE_T), 0) == grp
        onehot_f = jnp.where(onehot, 1.0, 0.0)
        bi = lax.broadcasted_iota(jnp.int32, (128, 128), 0)
        bj = lax.broadcasted_iota(jnp.int32, (128, 128), 1)
        strict_upper = jnp.where(bi < bj, 1.0, 0.0).astype(BF16)
        running = jnp.zeros((N_GROUPS, 1), F32)
        ranks = []
        for blk in range(MOE_T // 128):
            oh_b = onehot_f[:, 128 * blk:128 * (blk + 1)]
            ranks.append(_dot(oh_b.astype(BF16), strict_upper) + running)
            running = running + jnp.sum(oh_b, axis=1, keepdims=True)
        rank = jnp.concatenate(ranks, axis=1)
        tiles = lax.shift_right_logical(running.astype(jnp.int32) + (MOE_TILE - 1), 7)
        starts = [jnp.zeros((1, 1), jnp.int32)]
        for g in range(1, N_GROUPS):
            starts.append(starts[-1] + tiles[g - 1:g, :])
        used = starts[-1] + tiles[N_GROUPS - 1:, :]
        start = jnp.concatenate(starts, axis=0)
        pos = jnp.sum(jnp.where(onehot, (start * MOE_TILE).astype(F32) + rank, 0.0), axis=0, keepdims=True)
        pos = pos.astype(jnp.int32)
        for g in range(N_GROUPS):
            seg_ref[g] = jnp.sum(start[g:g + 1, :])
            seg_ref[N_GROUPS + g] = jnp.sum(tiles[g:g + 1, :])

        def fill(i, c):
            off = pl.multiple_of(i * MOE_TILE, MOE_TILE)
            slot = lax.broadcasted_iota(jnp.int32, (MOE_TILE, MOE_T), 0) + off
            p_ref[pl.ds(off, MOE_TILE), :] = jnp.where(slot == pos, 1.0, 0.0).astype(BF16)
            return c
        lax.fori_loop(0, n_tiles, fill, 0)

        p = p_ref[...]
        for cb in range(D_MODEL // MOE_COL):
            cols = slice(MOE_COL * cb, MOE_COL * (cb + 1))
            xs_ref[:, cols] = _dot(p, xm_ref[:, cols]).astype(BF16)
        ct = combt_ref[...]
        hi = ct.astype(BF16)
        r1 = ct - hi.astype(F32)
        mid = r1.astype(BF16)
        lo = (r1 - mid.astype(F32)).astype(BF16)
        nt = lambda a: lax.dot_general(a, p, (((1,), (1,)), ((), ())), preferred_element_type=F32)
        cs_ref[...] = (nt(hi) + nt(mid) + nt(lo)).T

        def clear(i, c):
            off = pl.multiple_of(i * MOE_TILE, MOE_TILE)
            y_ref[pl.ds(off, MOE_TILE), :] = jnp.zeros((MOE_TILE, D_MODEL), F32)
            return c
        lax.fori_loop(jnp.sum(used), n_tiles, clear, 0)

    g_e = lax.shift_right_logical(e, 2)
    first_tile = seg_ref[g_e]
    group_tiles = seg_ref[N_GROUPS + g_e]
    first_in_group = (e & (EXPERTS_PER_GROUP - 1)) == 0
    lane = lax.broadcasted_iota(jnp.int32, (1, N_EXPERTS), 1)

    def expert_rows(tile, rows):
        off = pl.multiple_of(tile * MOE_TILE, MOE_TILE)
        xs = xs_ref[pl.ds(off, rows), :]
        h = _silu(_dot(xs, wg_ref[0])) * _dot(xs, wu_ref[0])
        y = _dot(h.astype(BF16), wd_ref[0])
        w_e = jnp.sum(jnp.where(lane == e, cs_ref[pl.ds(off, rows), :], 0.0), axis=-1, keepdims=True)

        @pl.when(first_in_group)
        def _():
            y_ref[pl.ds(off, rows), :] = w_e * y

        @pl.when(jnp.logical_not(first_in_group))
        def _():
            y_ref[pl.ds(off, rows), :] += w_e * y

    def two_tiles(i, c):
        expert_rows(first_tile + 2 * i, 2 * MOE_TILE)
        return c
    lax.fori_loop(0, lax.shift_right_logical(group_tiles, 1), two_tiles, 0)

    @pl.when((group_tiles & 1) == 1)
    def _():
        expert_rows(first_tile + group_tiles - 1, MOE_TILE)

    @pl.when(e == N_EXPERTS - 1)
    def _():
        p = p_ref[...]
        for cb in range(D_MODEL // MOE_COL):
            cols = slice(MOE_COL * cb, MOE_COL * (cb + 1))
            o_ref[:, cols] = lax.dot_general(p, y_ref[:, cols].astype(BF16), (((0,), (0,)), ((), ())),
                                             preferred_element_type=F32)
        mod = mod_ref[0]
        o_ref[...] = _layer_norm(ALPHA * x1_ref[...] + mod[5:6] * o_ref[...], g_ref[...], b_ref[...])


def _moe_call(xm2, combt, grp, wg, wu, wd, x1, mod, ln_g, ln_b, seq_len):
    n = xm2.shape[0]
    per_seq = max(seq_len // MOE_T, 1)
    mod_idx = (lambda i, e: (i // per_seq, 0, 0)) if mod.shape[0] > 1 else (lambda i, e: (0, 0, 0))
    row = lambda w: pl.BlockSpec((MOE_T, w), lambda i, e: (i, 0))
    col = lambda h: pl.BlockSpec((h, MOE_T), lambda i, e: (0, i))
    return pl.pallas_call(
        _moe_kernel,
        grid=(n // MOE_T, N_EXPERTS),
        in_specs=[row(D_MODEL), col(N_EXPERTS), col(1),
                  pl.BlockSpec((1, D_MODEL, D_FF), lambda i, e: (e, 0, 0)),
                  pl.BlockSpec((1, D_MODEL, D_FF), lambda i, e: (e, 0, 0)),
                  pl.BlockSpec((1, D_FF, D_MODEL), lambda i, e: (e, 0, 0)),
                  row(D_MODEL),
                  pl.BlockSpec((1, 8, D_MODEL), mod_idx),
                  pl.BlockSpec((1, D_MODEL), lambda i, e: (0, 0)),
                  pl.BlockSpec((1, D_MODEL), lambda i, e: (0, 0))],
        out_specs=row(D_MODEL),
        out_shape=jax.ShapeDtypeStruct((n, D_MODEL), F32),
        scratch_shapes=[pltpu.VMEM((MOE_SLOTS, MOE_T), BF16),
                        pltpu.VMEM((MOE_SLOTS, D_MODEL), BF16),
                        pltpu.VMEM((MOE_SLOTS, N_EXPERTS), F32),
                        pltpu.VMEM((MOE_SLOTS, D_MODEL), F32),
                        pltpu.SMEM((2 * N_GROUPS,), jnp.int32)],
        compiler_params=_params(("arbitrary", "arbitrary")),
        name="moe_grouped",
    )(xm2, combt, grp, wg, wu, wd, x1, mod, ln_g.reshape(1, D_MODEL), ln_b.reshape(1, D_MODEL))


def _rot_weight(w, dim):
    nf, m = dim // 4, dim // 2
    blocks = w.reshape(w.shape[0], -1, dim)
    rot = jnp.concatenate([-blocks[..., nf:m], blocks[..., :nf], -blocks[..., m + nf:], blocks[..., m:m + nf]],
                          axis=-1)
    return rot.reshape(w.shape)


def _pack_in_proj(w_in_l, rope):
    z, xbc, dtw, da, ret = jnp.split(w_in_l, (512, 1280, 1296, 2064), axis=1)
    wq, wk, wv = jnp.split(da, 3, axis=1)
    wrq, wrk, wrv, wrg = jnp.split(ret, 4, axis=1)
    nn_w = {"z": z, "xbc": xbc, "k": wk, "rq": wrq, "rv": wrv, "rg": wrg, "v": wv,
            "dt": jnp.pad(dtw, ((0, 0), (0, DT_COLS - dtw.shape[1])))}
    nt_w = {"q": wq.T, "v": wv.T, "rk": wrk.T, "dt": jnp.pad(dtw.T, ((0, DT_ROWS - dtw.shape[1]), (0, 0)))}
    if rope:
        nn_w.update(k_rot=_rot_weight(wk, DA_HALF), rq_rot=_rot_weight(wrq, RET_HEAD_DIM))
        nt_w.update(q_rot=_rot_weight(wq, DA_HALF).T, rk_rot=_rot_weight(wrk, RET_HEAD_DIM).T)
    cols, rows = _in_proj_layout(rope)
    return (jnp.concatenate([nn_w[name] for name in cols], axis=1).astype(BF16),
            jnp.concatenate([nt_w[name] for name in rows], axis=0).astype(BF16))


def _rope_tables(seq_len, dim, reps):
    nf = dim // 4
    inv = ROPE_BASE ** (-jnp.arange(nf, dtype=F32) / nf)
    t = jnp.arange(seq_len)
    r = (t // GRID_W).astype(F32)[:, None] * inv
    c = (t % GRID_W).astype(F32)[:, None] * inv
    ang = jnp.tile(jnp.concatenate([r, r, c, c], axis=-1), (1, reps))
    return jnp.cos(ang), jnp.sin(ang)


def _layer(x, mod, l, P, ctx, rope_tabs):
    bsz, seq_len, _ = x.shape
    n = bsz * seq_len
    x2d = x.reshape(n, D_MODEL)
    latent = ctx is not None
    wnn, wnt = P["in_proj"][l][1 if latent else 0]
    outs = _in_proj_call(x2d, mod, wnn, wnt, seq_len, rope_tabs)
    z, xbc, dt, dtt, qt, k, vt, rq, rkt, rv, rg = outs[:11]
    r3 = lambda a: a.reshape(bsz, seq_len, a.shape[-1])

    h_ssd0 = ctx[2] if latent else None
    h_ret0 = ctx[3] if latent else None
    ssd_out = _ssd_call(r3(xbc), r3(z), r3(dt), dtt, P["ssd_conv_w"][l], P["ssd_conv_b"][l],
                        P["ssd_dt_bias"][l], P["ssd_a_log"][l], P["ssd_d"][l], P["ssd_norm_w"][l], h_ssd0)
    ret_out = _ret_call(r3(rq), rkt, r3(rv), r3(rg), P["ret_decay"][l], P["ret_norm_w"][l], h_ret0)

    ks, vts = [r3(k)], [vt]
    if latent:
        y_ssd, y_ret = ssd_out, ret_out
        new = None
        ks.append(ctx[0].reshape(bsz, -1, D_DA).astype(BF16))
        vts.append(jnp.transpose(ctx[1].reshape(bsz, -1, D_DA), (0, 2, 1)).astype(BF16))
    else:
        y_ssd, hs = ssd_out
        y_ret, hr = ret_out
        v = outs[11]
        new = (k.reshape(bsz, seq_len, H_DA, DA_HEAD_DIM), v.reshape(bsz, seq_len, H_DA, DA_HEAD_DIM), hs, hr)
    lam_init = 0.8 - 0.6 * math.exp(-0.3 * l)
    y_da = _da_call(qt, ks, vts, P["da_lambda"][l], P["da_norm_w"][l], lam_init)

    x1, xm2, combt, grp = _out_proj_call(y_ssd.reshape(n, D_SSD), y_da.reshape(n, D_DA),
                                         y_ret.reshape(n, D_RET), x2d, mod, P["w_out16"][l],
                                         P["ln_mix_g"][l], P["ln_mix_b"][l], P["router_w"], P["router_b"],
                                         seq_len)
    x2 = _moe_call(xm2, combt, grp, P["wg16"][l], P["wu16"][l], P["wd16"][l], x1, mod,
                   P["ln_ffn_g"][l], P["ln_ffn_b"][l], seq_len)
    return x2.reshape(bsz, seq_len, D_MODEL), new


def kernel(x_prompt, x_sample, c, cache_da_k, cache_da_v, state_ssd, state_ret, c_ctx, w_ada, b_ada, w_in,
           ssd_conv_w, ssd_conv_b, ssd_dt_bias, ssd_a_log, ssd_d, ssd_norm_w, da_lambda, da_norm_w, ret_decay,
           ret_norm_w, w_out, ln_mix_g, ln_mix_b, router_w, router_b, moe_w_gate, moe_w_up, moe_w_down,
           ln_ffn_g, ln_ffn_b):
    dec_b = x_sample.shape[0]
    P = dict(ssd_conv_w=ssd_conv_w, ssd_conv_b=ssd_conv_b, ssd_dt_bias=ssd_dt_bias, ssd_a_log=ssd_a_log,
             ssd_d=ssd_d, ssd_norm_w=ssd_norm_w, da_lambda=da_lambda, da_norm_w=da_norm_w,
             ret_decay=ret_decay, ret_norm_w=ret_norm_w, ln_mix_g=ln_mix_g, ln_mix_b=ln_mix_b,
             router_w=router_w, router_b=router_b, ln_ffn_g=ln_ffn_g, ln_ffn_b=ln_ffn_b)
    P["in_proj"] = [(_pack_in_proj(w_in[l], False), _pack_in_proj(w_in[l], True)) for l in range(DEPTH)]
    P["w_out16"] = w_out.astype(BF16)
    P["wg16"] = moe_w_gate.astype(BF16)
    P["wu16"] = moe_w_up.astype(BF16)
    P["wd16"] = moe_w_down.astype(BF16)

    cond8 = jnp.concatenate([c_ctx[None, :], c, jnp.zeros((8 - 1 - dec_b, D_MODEL), F32)], axis=0)
    mod_all = _ada_call(cond8, w_ada, b_ada).reshape(DEPTH, 8, 6, D_MODEL)
    mod_all = jnp.pad(mod_all, ((0, 0), (0, 0), (0, 2), (0, 0)))

    y = x_prompt
    ks_, vs_, hs_, hr_ = [], [], [], []
    for l in range(DEPTH):
        y, (k_l, v_l, hs_l, hr_l) = _layer(y, mod_all[l, 0:1], l, P, None, None)
        ks_.append(k_l)
        vs_.append(v_l)
        hs_.append(hs_l)
        hr_.append(hr_l)
    new_da_k = jnp.stack(ks_, axis=1)
    new_da_v = jnp.stack(vs_, axis=1)
    new_ssd = jnp.stack(hs_, axis=1)
    new_ret = jnp.stack(hr_, axis=1)

    seq_len = x_sample.shape[1]
    cos_da, sin_da = _rope_tables(seq_len, DA_HALF, D_DA // DA_HALF)
    cos_ret, sin_ret = _rope_tables(seq_len, RET_HEAD_DIM, H_RET)
    rope_tabs = (cos_da, sin_da, cos_ret, sin_ret, cos_da.T, sin_da.T, cos_ret.T, sin_ret.T)
    zl = x_sample
    for l in range(DEPTH):
        ctx = (cache_da_k[:, l], cache_da_v[:, l], state_ssd[:, l], state_ret[:, l])
        zl, _ = _layer(zl, mod_all[l, 1:1 + dec_b], l, P, ctx, rope_tabs)

    return (y, zl, new_da_k, new_da_v, new_ssd, new_ret)
```

```python
import functools
import math

import jax
import jax.numpy as jnp
import numpy as np
from jax import lax
from jax.experimental import pallas as pl
from jax.experimental.pallas import tpu as pltpu

D_MODEL = 1024
DEPTH = 2
GRID_W = 64
CHUNK = 128
H_SSD = 8
SSD_HEAD_DIM = 64
D_SSD = H_SSD * SSD_HEAD_DIM
SSD_GROUPS = 2
D_STATE = 64
CONV_W = 5
D_XBC = D_SSD + 2 * SSD_GROUPS * D_STATE
H_DA = 4
DA_HALF = 32
DA_HEAD_DIM = 2 * DA_HALF
D_DA = H_DA * DA_HEAD_DIM
H_RET = 4
RET_HEAD_DIM = 64
D_RET = H_RET * RET_HEAD_DIM
N_EXPERTS = 16
EXPERTS_PER_GROUP = 4
N_GROUPS = N_EXPERTS // EXPERTS_PER_GROUP
D_FF = 512
ROPE_BASE = 10000.0
EPS = 1e-6
ALPHA = (2 * DEPTH) ** 0.25

F32 = jnp.float32
BF16 = jnp.bfloat16
HIGHEST = lax.Precision.HIGHEST
NEG_INF = float("-inf")
LOG2_E = 1.4426950408889634

V7X_VMEM_LIMIT_BYTES = 56 * 1024 * 1024
CONV_HALO = 8

DT_COLS = 128
DT_ROWS = 32


def _in_proj_layout(rope):
    nn = [("z", D_SSD), ("xbc", D_XBC), ("k", D_DA), ("rq", D_RET), ("rv", D_RET), ("rg", D_RET), ("dt", DT_COLS)]
    nt = [("q", D_DA), ("v", D_DA), ("rk", D_RET), ("dt", DT_ROWS)]
    if not rope:
        nn += [("v", D_DA)]

    def spans(parts):
        out, at = {}, 0
        for name, width in parts:
            out[name] = (at, at + width)
            at += width
        return out
    return spans(nn), spans(nt)


def _silu(x):
    return x * (1.0 / (1.0 + jnp.exp(-x)))


def _softplus(x):
    return jnp.maximum(x, 0.0) + jnp.log1p(jnp.exp(-jnp.abs(x)))


def _dot(a, b, precision=None):
    return jnp.dot(a, b, preferred_element_type=F32, precision=precision)


def _layer_norm(t, g, b):
    mu = jnp.mean(t, axis=-1, keepdims=True)
    tc = t - mu
    var = jnp.mean(tc * tc, axis=-1, keepdims=True)
    return tc * lax.rsqrt(var + EPS) * g + b


def _params(sem):
    return pltpu.CompilerParams(dimension_semantics=sem, vmem_limit_bytes=V7X_VMEM_LIMIT_BYTES)


def _ada_kernel(cond_ref, w_ref, b_ref, o_ref):
    o_ref[0] = _dot(_silu(cond_ref[...]), w_ref[0], HIGHEST) + b_ref[0]


def _ada_call(cond8, w_ada, b_ada):
    tn = 1536
    nb = (6 * D_MODEL) // tn
    return pl.pallas_call(
        _ada_kernel,
        grid=(DEPTH, nb),
        in_specs=[pl.BlockSpec((8, D_MODEL), lambda l, n: (0, 0)),
                  pl.BlockSpec((1, D_MODEL, tn), lambda l, n: (l, 0, n)),
                  pl.BlockSpec((1, 1, tn), lambda l, n: (l, 0, n))],
        out_specs=pl.BlockSpec((1, 8, tn), lambda l, n: (l, 0, n)),
        out_shape=jax.ShapeDtypeStruct((DEPTH, 8, 6 * D_MODEL), F32),
        compiler_params=_params(("arbitrary", "arbitrary")),
        name="ada_mod",
    )(cond8, w_ada, b_ada.reshape(DEPTH, 1, 6 * D_MODEL))


def _rot_rows(x, dim):
    nf, m = dim // 4, dim // 2
    parts = []
    for b in range(0, x.shape[0], dim):
        parts += [-x[b + nf:b + m], x[b:b + nf], -x[b + m + nf:b + dim], x[b + m:b + m + nf]]
    return jnp.concatenate(parts, axis=0)


def _rot_lanes(x, dim):
    nf, n = dim // 4, x.shape[1]
    lane = lax.broadcasted_iota(jnp.int32, (1, n), 1)
    ahead = pltpu.roll(x, n - nf, axis=1)
    behind = pltpu.roll(x, nf, axis=1)
    return jnp.where((lane & (2 * nf - 1)) < nf, -ahead, behind)


def _in_proj_kernel(rope, *refs):
    if rope:
        (x_ref, mod_ref, wnn_ref, wnt_ref, cd_ref, sd_ref, cr_ref, sr_ref,
         cdt_ref, sdt_ref, crt_ref, srt_ref,
         z_ref, xbc_ref, dt_ref, dtt_ref, qt_ref, k_ref, vt_ref, rq_ref, rkt_ref, rv_ref, rg_ref) = refs
    else:
        (x_ref, mod_ref, wnn_ref, wnt_ref,
         z_ref, xbc_ref, dt_ref, dtt_ref, qt_ref, k_ref, vt_ref, rq_ref, rkt_ref, rv_ref, rg_ref,
         v_ref) = refs
    cols, rows = _in_proj_layout(rope)
    mod = mod_ref[0]
    xm = (x_ref[...] * (1.0 + mod[1:2]) + mod[0:1]).astype(BF16)

    def nn(name):
        a, b = cols[name]
        return _dot(xm, wnn_ref[:, a:b])

    def nt(name):
        a, b = rows[name]
        return lax.dot_general(wnt_ref[a:b, :], xm, (((1,), (1,)), ((), ())),
                               preferred_element_type=F32)

    z_ref[...] = nn("z")
    xbc_ref[...] = nn("xbc")
    dt_ref[...] = nn("dt")[:, :2 * H_SSD]
    dtt_ref[0] = nt("dt")[:2 * H_SSD, :]
    vt_ref[0] = nt("v").astype(BF16)
    rv_ref[...] = nn("rv")
    rg_ref[...] = nn("rg")
    qt = nt("q")
    k = nn("k")
    rq = nn("rq")
    rkt = nt("rk")
    if rope:
        qt = qt * cdt_ref[...] + _rot_rows(qt, DA_HALF) * sdt_ref[...]
        k = k * cd_ref[...] + _rot_lanes(k, DA_HALF) * sd_ref[...]
        rq = rq * cr_ref[...] + _rot_lanes(rq, RET_HEAD_DIM) * sr_ref[...]
        rkt = rkt * crt_ref[...] + _rot_rows(rkt, RET_HEAD_DIM) * srt_ref[...]
    else:
        v_ref[...] = nn("v")
    qt_ref[0] = (qt * (DA_HALF ** -0.5 * LOG2_E)).astype(BF16)
    k_ref[...] = k.astype(k_ref.dtype)
    rq_ref[...] = (rq * (RET_HEAD_DIM ** -0.5)).astype(BF16)
    rkt_ref[0] = rkt.astype(BF16)


def _in_proj_call(x2d, mod, wnn, wnt, seq_len, rope_tabs):
    n = x2d.shape[0]
    tm = 256
    nblk = n // tm
    per_seq = seq_len // tm
    n_mod = mod.shape[0]
    rope = rope_tabs is not None
    mod_idx = (lambda i: (i // per_seq, 0, 0)) if n_mod > 1 else (lambda i: (0, 0, 0))
    bsz = n // seq_len
    row = lambda w: pl.BlockSpec((tm, w), lambda i: (i, 0))
    col = lambda h: pl.BlockSpec((1, h, tm), lambda i: (i // per_seq, 0, i % per_seq))
    in_specs = [row(D_MODEL),
                pl.BlockSpec((1, 8, D_MODEL), mod_idx),
                pl.BlockSpec(wnn.shape, lambda i: (0, 0)),
                pl.BlockSpec(wnt.shape, lambda i: (0, 0))]
    args = [x2d, mod, wnn, wnt]
    if rope:
        in_specs += [pl.BlockSpec((tm, 256), lambda i: (i % per_seq, 0))] * 4
        in_specs += [pl.BlockSpec((256, tm), lambda i: (0, i % per_seq))] * 4
        args += list(rope_tabs)
    sds = jax.ShapeDtypeStruct
    out_shape = [sds((n, D_SSD), F32), sds((n, D_XBC), F32), sds((n, 2 * H_SSD), F32),
                 sds((bsz, 2 * H_SSD, seq_len), F32), sds((bsz, D_DA, seq_len), BF16),
                 sds((n, D_DA), BF16 if rope else F32), sds((bsz, D_DA, seq_len), BF16),
                 sds((n, D_RET), BF16), sds((bsz, D_RET, seq_len), BF16),
                 sds((n, D_RET), F32), sds((n, D_RET), F32)]
    out_specs = [row(D_SSD), row(D_XBC), row(2 * H_SSD), col(2 * H_SSD), col(D_DA), row(D_DA),
                 col(D_DA), row(D_RET), col(D_RET), row(D_RET), row(D_RET)]
    if not rope:
        out_shape.append(sds((n, D_DA), F32))
        out_specs.append(row(D_DA))
    return pl.pallas_call(
        functools.partial(_in_proj_kernel, rope),
        grid=(nblk,),
        in_specs=in_specs,
        out_specs=out_specs,
        out_shape=out_shape,
        compiler_params=_params(("arbitrary",)),
        name="in_proj_rope" if rope else "in_proj",
    )(*args)


SEQS_PER_STEP = 4


def _seqs_per_step(bsz):
    assert bsz % SEQS_PER_STEP == 0
    return SEQS_PER_STEP


SSD_SEQ_REFS = (0, 1, 2, 3, 4, 5)
RET_SEQ_REFS = (0, 1, 2, 3)


def _per_sequence(seq_kernel, n_in, seq_in, n_scratch, nc, nb, has_h0, *refs):
    ins = list(refs[:n_in])
    outs = list(refs[n_in:len(refs) - n_scratch])
    scratch = list(refs[len(refs) - n_scratch:])
    live = []
    for s in range(nb):
        one = lambda r, s=s: r.at[pl.ds(s, 1)]
        seq_ins = [one(r) if i in seq_in else r for i, r in enumerate(ins)]
        live.append(seq_kernel(nc, has_h0, *seq_ins, *[one(r) for r in outs], *[r.at[s] for r in scratch]))
    while live:
        still = []
        for gen in live:
            try:
                next(gen)
                still.append(gen)
            except StopIteration:
                pass
        live = still


CONV_SHIFT_TAPS = tuple(k for k in range(CONV_W) if k != CONV_W // 2)
CONV_EXT_ROWS = CHUNK + 2 * CONV_HALO
CONV_SHIFT_SHAPE = (len(CONV_SHIFT_TAPS) * CHUNK, 2 * CONV_EXT_ROWS)


def _conv_shift_matrix():
    s = np.zeros(CONV_SHIFT_SHAPE, np.float32)
    t = np.arange(CHUNK)
    for idx, k in enumerate(CONV_SHIFT_TAPS):
        src = CONV_HALO + t + k - CONV_W // 2
        s[idx * CHUNK + t, src] = 1.0
        s[idx * CHUNK + t, CONV_EXT_ROWS + src] = 1.0
    return jnp.asarray(s, dtype=BF16)


def _ssd_seq(nc, has_h0, *refs):
    refs = list(refs)
    (xm_ref, xp_ref, xn_ref, z_ref, dt_ref, dtt_ref, cw_ref, cb_ref, dtb_c_ref, dtb_r_ref,
     alog_c_ref, alog_r_ref, dskip_ref, nw_ref, shift_ref) = refs[:15]
    refs = refs[15:]
    h0_ref = refs.pop(0) if has_h0 else None
    y_ref = refs.pop(0)
    hs_ref = None if has_h0 else refs.pop(0)
    hp_ref, yf_ref = refs

    d = pl.program_id(1)
    j = pl.program_id(2)
    fwd = d == 0
    cj = jnp.where(fwd, j, nc - 1 - j)

    @pl.when(j == 0)
    def _():
        for p in range(H_SSD // 2):
            if has_h0:
                g = p // 2
                both = jnp.concatenate([h0_ref[0, pl.ds(d, 1), 2 * p][0],
                                        h0_ref[0, pl.ds(d, 1), 2 * p + 1][0]], axis=1)
                zero = jnp.zeros((D_STATE, 128), F32)
                hp_ref[p] = jnp.concatenate([both, zero] if g == 0 else [zero, both], axis=0)
            else:
                hp_ref[p] = jnp.zeros((128, 128), F32)
    yield

    x_c = xm_ref[0]
    ext = jnp.concatenate([jnp.where(cj > 0, xp_ref[0], 0.0), x_c, jnp.where(cj < nc - 1, xn_ref[0], 0.0)],
                          axis=0)
    hi = ext.astype(BF16)
    lo = (ext - hi.astype(F32)).astype(BF16)
    shifted = _dot(shift_ref[...], jnp.concatenate([hi, lo], axis=0))
    acc = cb_ref[...] + cw_ref[CONV_W // 2:CONV_W // 2 + 1, :] * x_c
    for idx, k in enumerate(CONV_SHIFT_TAPS):
        acc = acc + cw_ref[k:k + 1, :] * shifted[CHUNK * idx:CHUNK * (idx + 1), :]
    act = _silu(acc)
    xs = act[:, :D_SSD]
    bm = act[:, D_SSD:D_SSD + 128]
    cm = act[:, D_SSD + 128:]
    yield

    dt_c_all = _softplus(dt_ref[0] + dtb_c_ref[...])
    dt_r_all = _softplus(dtt_ref[0] + dtb_r_ref[...])
    a_c_all = -jnp.exp(alog_c_ref[...])
    a_r_all = -jnp.exp(alog_r_ref[...])
    dt_c = jnp.where(fwd, dt_c_all[:, :H_SSD], dt_c_all[:, H_SSD:])
    dt_r = jnp.where(fwd, dt_r_all[:H_SSD], dt_r_all[H_SSD:])
    la_c = dt_c * jnp.where(fwd, a_c_all[:, :H_SSD], a_c_all[:, H_SSD:])
    la_r = dt_r * jnp.where(fwd, a_r_all[:H_SSD], a_r_all[H_SSD:])

    ii = lax.broadcasted_iota(jnp.int32, (CHUNK, CHUNK), 0)
    jj = lax.broadcasted_iota(jnp.int32, (CHUNK, CHUNK), 1)
    valid = jnp.where(fwd, ii - jj, jj - ii) >= 0
    valid_t = jnp.where(fwd, jj - ii, ii - jj) >= 0
    cum_c = _dot(valid.astype(F32), la_c, HIGHEST)
    cum_r = _dot(la_r, valid_t.astype(F32), HIGHEST)
    tot_r = jnp.where(fwd, cum_r[:, CHUNK - 1:], cum_r[:, :1])
    e_tot = jnp.exp(tot_r)
    wk_r = dt_r * jnp.exp(tot_r - cum_r)
    e_cum_c = jnp.exp(cum_c)

    yield
    bm_t = bm.T
    bm_t16 = bm_t.astype(BF16)
    lane = lax.broadcasted_iota(jnp.int32, (1, 128), 1)
    low_lanes = lane < 64
    feat_row = lax.broadcasted_iota(jnp.int32, (128, 1), 0)

    ys = []
    for g in range(SSD_GROUPS):
        in_group_lane = (lane >= 64 * g) & (lane < 64 * (g + 1))
        cg = jnp.where(in_group_lane, cm, 0.0)
        gram = _dot(cg.astype(BF16), bm_t16)
        in_group_row = (feat_row >= 64 * g) & (feat_row < 64 * (g + 1))
        for p in range(2 * g, 2 * g + 2):
            xs_p = xs[:, 128 * p:128 * (p + 1)].astype(BF16)
            h_prev = hp_ref[p]
            h_prev16 = h_prev.astype(BF16)
            y_h, s_h = [], []
            for hh in (2 * p, 2 * p + 1):
                decay = jnp.exp(jnp.where(valid, cum_c[:, hh:hh + 1] - cum_r[hh:hh + 1, :], NEG_INF))
                sc = gram * decay * dt_r[hh:hh + 1, :]
                y_intra = _dot(sc.astype(BF16), xs_p)
                y_inter = _dot((cg * e_cum_c[:, hh:hh + 1]).astype(BF16), h_prev16)
                y_h.append(y_intra + y_inter)
                s_h.append(_dot((bm_t * wk_r[hh:hh + 1, :]).astype(BF16), xs_p))
            ys.append(jnp.where(low_lanes, y_h[0], y_h[1]))
            s_pair = jnp.where(low_lanes, s_h[0], s_h[1])
            e_pair = jnp.where(low_lanes, e_tot[2 * p:2 * p + 1, :], e_tot[2 * p + 1:2 * p + 2, :])
            hp_ref[p] = jnp.where(in_group_row, e_pair * h_prev + s_pair, 0.0)
            yield
    y_dir = jnp.concatenate(ys, axis=1)

    if hs_ref is not None:
        @pl.when(j == nc - 1)
        def _():
            for dd in range(2):
                @pl.when(d == dd)
                def _():
                    for p in range(H_SSD // 2):
                        g = p // 2
                        hn = hp_ref[p]
                        hs_ref[0, dd, 2 * p] = hn[64 * g:64 * (g + 1), :64]
                        hs_ref[0, dd, 2 * p + 1] = hn[64 * g:64 * (g + 1), 64:]

    @pl.when(fwd)
    def _():
        yf_ref[cj] = y_dir.astype(yf_ref.dtype)

    @pl.when(d == 1)
    def _():
        yt = yf_ref[cj].astype(F32) + y_dir + dskip_ref[...] * xs
        gated = yt * _silu(z_ref[0])
        ms = jnp.mean(gated * gated, axis=-1, keepdims=True)
        y_ref[0] = gated * lax.rsqrt(ms + EPS) * nw_ref[...]


def _ssd_call(xbc, z, dt, dtt, conv_w, conv_b, dt_bias, a_log, d_skip, norm_w, h0):
    bsz, seq_len, _ = xbc.shape
    nc = seq_len // CHUNK
    has_h0 = h0 is not None
    hb = CHUNK // CONV_HALO
    last_hb = seq_len // CONV_HALO - 1

    def chunk_of(d, j):
        return jnp.where(d == 0, j, nc - 1 - j)

    def full(shape):
        return pl.BlockSpec(shape, lambda b, d, j: (0,) * len(shape))

    nb = _seqs_per_step(bsz)
    in_specs = [
        pl.BlockSpec((nb, CHUNK, D_XBC), lambda b, d, j: (b, chunk_of(d, j), 0)),
        pl.BlockSpec((nb, CONV_HALO, D_XBC), lambda b, d, j: (b, jnp.maximum(chunk_of(d, j) * hb - 1, 0), 0)),
        pl.BlockSpec((nb, CONV_HALO, D_XBC),
                     lambda b, d, j: (b, jnp.minimum((chunk_of(d, j) + 1) * hb, last_hb), 0)),
        pl.BlockSpec((nb, CHUNK, D_SSD), lambda b, d, j: (b, chunk_of(d, j), 0)),
        pl.BlockSpec((nb, CHUNK, 2 * H_SSD), lambda b, d, j: (b, chunk_of(d, j), 0)),
        pl.BlockSpec((nb, 2 * H_SSD, CHUNK), lambda b, d, j: (b, 0, chunk_of(d, j))),
        full((CONV_W, D_XBC)), full((1, D_XBC)), full((1, 2 * H_SSD)), full((2 * H_SSD, 1)),
        full((1, 2 * H_SSD)), full((2 * H_SSD, 1)), full((1, D_SSD)), full((1, D_SSD)),
        full(CONV_SHIFT_SHAPE),
    ]
    args = [xbc, xbc, xbc, z, dt, dtt, conv_w, conv_b.reshape(1, D_XBC),
            dt_bias.reshape(1, 2 * H_SSD), dt_bias.reshape(2 * H_SSD, 1),
            a_log.reshape(1, 2 * H_SSD), a_log.reshape(2 * H_SSD, 1),
            jnp.repeat(d_skip, SSD_HEAD_DIM).reshape(1, D_SSD), norm_w.reshape(1, D_SSD),
            _conv_shift_matrix()]
    y_spec = pl.BlockSpec((nb, CHUNK, D_SSD), lambda b, d, j: (b, jnp.where(d == 0, nc - 1, nc - 1 - j), 0))
    y_shape = jax.ShapeDtypeStruct((bsz, seq_len, D_SSD), F32)
    state_block = (nb, 2, H_SSD, D_STATE, SSD_HEAD_DIM)
    state_spec = pl.BlockSpec(state_block, lambda b, d, j: (b, 0, 0, 0, 0))
    seq_in = SSD_SEQ_REFS
    if has_h0:
        seq_in = seq_in + (len(in_specs),)
        in_specs.append(state_spec)
        args.append(h0)
        out_specs, out_shape = y_spec, y_shape
    else:
        out_specs = [y_spec, state_spec]
        out_shape = [y_shape, jax.ShapeDtypeStruct((bsz,) + state_block[1:], F32)]
    return pl.pallas_call(
        functools.partial(_per_sequence, _ssd_seq, len(in_specs), seq_in, 2, nc, nb, has_h0),
        grid=(bsz // nb, 2, nc),
        in_specs=in_specs,
        out_specs=out_specs,
        out_shape=out_shape,
        scratch_shapes=[pltpu.VMEM((nb, H_SSD // 2, 128, 128), F32),
                        pltpu.VMEM((nb, nc, CHUNK, D_SSD), BF16)],
        compiler_params=_params(("arbitrary", "arbitrary", "arbitrary")),
        name="ssd_latent" if has_h0 else "ssd_ctx",
    )(*args)


def _ret_seq(nc, has_h0, *refs):
    refs = list(refs)
    q_ref, kt_ref, v_ref, g_ref, dec_ref, nw_ref = refs[:6]
    refs = refs[6:]
    h0_ref = refs.pop(0) if has_h0 else None
    y_ref = refs.pop(0)
    hs_ref = None if has_h0 else refs.pop(0)
    hr_ref, yf_ref = refs

    d = pl.program_id(1)
    j = pl.program_id(2)
    fwd = d == 0
    cj = jnp.where(fwd, j, nc - 1 - j)

    @pl.when(j == 0)
    def _():
        for p in range(H_RET // 2):
            if has_h0:
                zero = jnp.zeros((RET_HEAD_DIM, RET_HEAD_DIM), F32)
                top = jnp.concatenate([h0_ref[0, pl.ds(d, 1), 2 * p][0], zero], axis=1)
                bot = jnp.concatenate([zero, h0_ref[0, pl.ds(d, 1), 2 * p + 1][0]], axis=1)
                hr_ref[p] = jnp.concatenate([top, bot], axis=0)
            else:
                hr_ref[p] = jnp.zeros((128, 128), F32)
    yield

    dec = dec_ref[...]
    lg_all = -_softplus(-dec)
    lg = jnp.where(fwd, lg_all[0:1], lg_all[1:2])

    ii = lax.broadcasted_iota(jnp.int32, (CHUNK, CHUNK), 0)
    jj = lax.broadcasted_iota(jnp.int32, (CHUNK, CHUNK), 1)
    dist = jnp.where(fwd, ii - jj, jj - ii)
    valid = dist >= 0
    dist_f = dist.astype(F32)
    qpos = lax.broadcasted_iota(jnp.int32, (CHUNK, 1), 0)
    kpos = lax.broadcasted_iota(jnp.int32, (1, CHUNK), 1)
    n_q = jnp.where(fwd, qpos + 1, CHUNK - qpos).astype(F32)
    n_k = jnp.where(fwd, CHUNK - 1 - kpos, kpos).astype(F32)
    lane = lax.broadcasted_iota(jnp.int32, (1, 128), 1)
    low_lanes = lane < 64
    low_rows = lax.broadcasted_iota(jnp.int32, (128, 1), 0) < 64
    block_diag = low_rows == low_lanes

    ys = []
    for p in range(H_RET // 2):
        q_p = q_ref[0, :, 128 * p:128 * (p + 1)].astype(F32)
        kt_p = kt_ref[0, 128 * p:128 * (p + 1), :]
        v_p = v_ref[0, :, 128 * p:128 * (p + 1)].astype(BF16)
        h_prev = hr_ref[p]
        lg_a = lg[:, 2 * p:2 * p + 1]
        lg_b = lg[:, 2 * p + 1:2 * p + 2]
        y_h = []
        for lg_h, in_head in ((lg_a, low_lanes), (lg_b, jnp.logical_not(low_lanes))):
            s = _dot(jnp.where(in_head, q_p, 0.0).astype(BF16), kt_p)
            decay = jnp.exp(jnp.where(valid, dist_f * lg_h, NEG_INF))
            y_h.append(_dot((s * decay).astype(BF16), v_p))
        e_q = jnp.where(low_lanes, jnp.exp(n_q * lg_a), jnp.exp(n_q * lg_b))
        y_inter = _dot((q_p * e_q).astype(BF16), h_prev.astype(BF16))
        w_k = jnp.where(low_rows, jnp.exp(n_k * lg_a), jnp.exp(n_k * lg_b))
        s_new = _dot((kt_p.astype(F32) * w_k).astype(BF16), v_p)
        e_tot = jnp.where(low_lanes, jnp.exp(CHUNK * lg_a), jnp.exp(CHUNK * lg_b))
        hr_ref[p] = e_tot * h_prev + jnp.where(block_diag, s_new, 0.0)
        ys.append(jnp.where(low_lanes, y_h[0], y_h[1]) + y_inter)
        yield
    y_dir = jnp.concatenate(ys, axis=1)

    if hs_ref is not None:
        @pl.when(j == nc - 1)
        def _():
            for dd in range(2):
                @pl.when(d == dd)
                def _():
                    for p in range(H_RET // 2):
                        hn = hr_ref[p]
                        hs_ref[0, dd, 2 * p] = hn[:64, :64]
                        hs_ref[0, dd, 2 * p + 1] = hn[64:, 64:]

    @pl.when(fwd)
    def _():
        yf_ref[cj] = y_dir.astype(yf_ref.dtype)

    @pl.when(d == 1)
    def _():
        yt = yf_ref[cj].astype(F32) + y_dir
        lane4 = lax.shift_right_logical(lax.broadcasted_iota(jnp.int32, (1, D_RET), 1), 6)
        mu = jnp.zeros_like(yt)
        for h in range(H_RET):
            m_h = jnp.sum(jnp.where(lane4 == h, yt, 0.0), axis=-1, keepdims=True) * (1.0 / RET_HEAD_DIM)
            mu = jnp.where(lane4 == h, m_h, mu)
        yc = yt - mu
        sq = yc * yc
        rs = jnp.zeros_like(yt)
        for h in range(H_RET):
            v_h = jnp.sum(jnp.where(lane4 == h, sq, 0.0), axis=-1, keepdims=True) * (1.0 / RET_HEAD_DIM)
            rs = jnp.where(lane4 == h, lax.rsqrt(v_h + EPS), rs)
        y_ref[0] = _silu(g_ref[0]) * (yc * rs * nw_ref[...])


def _ret_call(rq, rkt, rv, rg, ret_decay, norm_w, h0):
    bsz, seq_len, _ = rq.shape
    nc = seq_len // CHUNK
    has_h0 = h0 is not None

    def chunk_of(d, j):
        return jnp.where(d == 0, j, nc - 1 - j)

    nb = _seqs_per_step(bsz)
    tok = pl.BlockSpec((nb, CHUNK, D_RET), lambda b, d, j: (b, chunk_of(d, j), 0))
    in_specs = [tok,
                pl.BlockSpec((nb, D_RET, CHUNK), lambda b, d, j: (b, 0, chunk_of(d, j))),
                tok, tok,
                pl.BlockSpec((2, H_RET), lambda b, d, j: (0, 0)),
                pl.BlockSpec((1, D_RET), lambda b, d, j: (0, 0))]
    args = [rq, rkt, rv, rg, ret_decay, jnp.tile(norm_w, H_RET).reshape(1, D_RET)]
    y_spec = pl.BlockSpec((nb, CHUNK, D_RET), lambda b, d, j: (b, jnp.where(d == 0, nc - 1, nc - 1 - j), 0))
    y_shape = jax.ShapeDtypeStruct((bsz, seq_len, D_RET), F32)
    state_block = (nb, 2, H_RET, RET_HEAD_DIM, RET_HEAD_DIM)
    state_spec = pl.BlockSpec(state_block, lambda b, d, j: (b, 0, 0, 0, 0))
    seq_in = RET_SEQ_REFS
    if has_h0:
        seq_in = seq_in + (len(in_specs),)
        in_specs.append(state_spec)
        args.append(h0)
        out_specs, out_shape = y_spec, y_shape
    else:
        out_specs = [y_spec, state_spec]
        out_shape = [y_shape, jax.ShapeDtypeStruct((bsz,) + state_block[1:], F32)]
    return pl.pallas_call(
        functools.partial(_per_sequence, _ret_seq, len(in_specs), seq_in, 2, nc, nb, has_h0),
        grid=(bsz // nb, 2, nc),
        in_specs=in_specs,
        out_specs=out_specs,
        out_shape=out_shape,
        scratch_shapes=[pltpu.VMEM((nb, H_RET // 2, 128, 128), F32),
                        pltpu.VMEM((nb, nc, CHUNK, D_RET), BF16)],
        compiler_params=_params(("arbitrary", "arbitrary", "arbitrary")),
        name="ret_latent" if has_h0 else "ret_ctx",
    )(*args)


DA_QUERY_BLOCK = 512
DA_KEY_CHUNK = 256
DA_SUM_ROWS = 16


def _da_kernel(lam_init, seg_lens, *refs):
    n_seg = len(seg_lens)
    qt_ref = refs[0]
    k_refs = refs[1:1 + n_seg]
    vt_refs = refs[1 + n_seg:1 + 2 * n_seg]
    lamp_ref, nw_ref, o_ref, s_scr, p_scr, va_scr = refs[1 + 2 * n_seg:]
    tq = qt_ref.shape[2]
    lk = sum(seg_lens)
    va_rows = DA_HEAD_DIM + DA_SUM_ROWS

    @pl.when(pl.program_id(1) == 0)
    def _():
        for h in range(H_DA):
            col = 0
            for seg in range(n_seg):
                keys = slice(col, col + seg_lens[seg])
                va_scr[h, 0:DA_HEAD_DIM, keys] = vt_refs[seg][0, DA_HEAD_DIM * h:DA_HEAD_DIM * (h + 1), :]
                va_scr[h, DA_HEAD_DIM:va_rows, keys] = jnp.ones((DA_SUM_ROWS, seg_lens[seg]), BF16)
                col += seg_lens[seg]

    lp = lamp_ref[...]
    s1 = jnp.sum(lp[0:1] * lp[1:2], axis=-1, keepdims=True)
    s2 = jnp.sum(lp[2:3] * lp[3:4], axis=-1, keepdims=True)
    lam = jnp.exp(s1) - jnp.exp(s2) + lam_init
    qt = qt_ref[0]
    half_of_row = lax.shift_right_logical(lax.broadcasted_iota(jnp.int32, (D_DA, 1), 0), 5)
    chunks = [(seg, DA_KEY_CHUNK * c) for seg in range(n_seg) for c in range(seg_lens[seg] // DA_KEY_CHUNK)]
    n_soft = 2 * H_DA
    q_half, m8 = {}, {}

    def scores(j, ci):
        seg, start = chunks[ci]
        if j not in q_half:
            q_half[j] = jnp.where(half_of_row == j, qt, jnp.zeros_like(qt))
        k_c = k_refs[seg][0, start:start + DA_KEY_CHUNK, :].astype(BF16)
        s = _dot(k_c, q_half[j])
        s_scr[j % 2, DA_KEY_CHUNK * ci:DA_KEY_CHUNK * (ci + 1), :] = s
        for r in range(DA_KEY_CHUNK // 8):
            part = s[8 * r:8 * (r + 1), :]
            m8[j] = part if j not in m8 else jnp.maximum(m8[j], part)

    for ci in range(len(chunks)):
        scores(0, ci)
    o_heads, pv = [], []
    for j in range(n_soft):
        m = jnp.max(m8[j], axis=0, keepdims=True)
        for ci in range(len(chunks)):
            if j + 1 < n_soft:
                scores(j + 1, ci)
            keys = slice(DA_KEY_CHUNK * ci, DA_KEY_CHUNK * (ci + 1))
            p_scr[j % 2, keys, :] = jnp.exp2(s_scr[j % 2, keys, :] - m).astype(BF16)
        pv.append(_dot(va_scr[j // 2], p_scr[j % 2]))
        if j % 2 == 1:
            inv0 = 1.0 / pv[0][DA_HEAD_DIM:DA_HEAD_DIM + 1, :]
            inv1 = 1.0 / pv[1][DA_HEAD_DIM:DA_HEAD_DIM + 1, :]
            o_h = pv[0][:DA_HEAD_DIM, :] * inv0 - pv[1][:DA_HEAD_DIM, :] * (lam * inv1)
            ms = jnp.mean(o_h * o_h, axis=0, keepdims=True)
            o_heads.append(o_h * lax.rsqrt(ms + EPS) * nw_ref[...] * (1.0 - lam_init))
            pv = []
    o_ref[0] = jnp.concatenate(o_heads, axis=0).T


def _da_call(qt, ks, vts, lam_params, norm_w, lam_init):
    bsz, _, lq = qt.shape
    seg_lens = tuple(k.shape[1] for k in ks)
    lk = sum(seg_lens)
    tq = min(DA_QUERY_BLOCK, lq)
    in_specs = [pl.BlockSpec((1, D_DA, tq), lambda b, i: (b, 0, i))]
    in_specs += [pl.BlockSpec((1, n, D_DA), lambda b, i: (b, 0, 0)) for n in seg_lens]
    in_specs += [pl.BlockSpec((1, D_DA, n), lambda b, i: (b, 0, 0)) for n in seg_lens]
    in_specs += [pl.BlockSpec((4, DA_HALF), lambda b, i: (0, 0)),
                 pl.BlockSpec((DA_HEAD_DIM, 1), lambda b, i: (0, 0))]
    return pl.pallas_call(
        functools.partial(_da_kernel, lam_init, seg_lens),
        grid=(bsz, lq // tq),
        in_specs=in_specs,
        out_specs=pl.BlockSpec((1, tq, D_DA), lambda b, i: (b, i, 0)),
        out_shape=jax.ShapeDtypeStruct((bsz, lq, D_DA), F32),
        scratch_shapes=[pltpu.VMEM((2, lk, tq), F32),
                        pltpu.VMEM((2, lk, tq), BF16),
                        pltpu.VMEM((H_DA, DA_HEAD_DIM + DA_SUM_ROWS, lk), BF16)],
        compiler_params=_params(("arbitrary", "arbitrary")),
        name="diff_attn",
    )(qt, *ks, *vts, lam_params, norm_w.reshape(DA_HEAD_DIM, 1))


def _out_proj_kernel(ys_ref, yd_ref, yr_ref, x_ref, mod_ref, w_ref, g_ref, b_ref, rwt_ref, rb_ref,
                     x1_ref, xm2_ref, combt_ref, grp_ref):
    mix = (_dot(ys_ref[...].astype(BF16), w_ref[0:D_SSD, :])
           + _dot(yd_ref[...].astype(BF16), w_ref[D_SSD:D_SSD + D_DA, :])
           + _dot(yr_ref[...].astype(BF16), w_ref[D_SSD + D_DA:, :]))
    mod = mod_ref[0]
    x1 = _layer_norm(ALPHA * x_ref[...] + mod[2:3] * mix, g_ref[...], b_ref[...])
    x1_ref[...] = x1
    xm2 = x1 * (1.0 + mod[4:5]) + mod[3:4]
    xm2_ref[...] = xm2.astype(BF16)

    logits = lax.dot_general(rwt_ref[...], xm2, (((1,), (1,)), ((), ())),
                             preferred_element_type=F32, precision=HIGHEST)
    ex = jnp.exp(logits - jnp.max(logits, axis=0, keepdims=True))
    scores = ex / jnp.sum(ex, axis=0, keepdims=True)
    sel = scores + rb_ref[...]
    row = lax.broadcasted_iota(jnp.int32, sel.shape, 0)
    row_f = row.astype(F32)
    gs = []
    for g in range(N_GROUPS):
        v = [sel[EXPERTS_PER_GROUP * g + i:EXPERTS_PER_GROUP * g + i + 1, :] for i in range(EXPERTS_PER_GROUP)]
        pair_sums = [v[a] + v[b] for a in range(EXPERTS_PER_GROUP) for b in range(a + 1, EXPERTS_PER_GROUP)]
        gs.append(functools.reduce(jnp.maximum, pair_sums))
    best = functools.reduce(jnp.maximum, gs)
    grp = jnp.full(best.shape, N_GROUPS - 1, jnp.int32)
    for g in range(N_GROUPS - 2, -1, -1):
        grp = jnp.where(gs[g] == best, g, grp)
    masked = jnp.where(lax.shift_right_logical(row, 2) == grp, sel, NEG_INF)
    m1 = jnp.max(masked, axis=0, keepdims=True)
    i1 = jnp.min(jnp.where(masked == m1, row_f, float(N_EXPERTS)), axis=0, keepdims=True)
    rest = jnp.where(row_f == i1, NEG_INF, masked)
    m2 = jnp.max(rest, axis=0, keepdims=True)
    i2 = jnp.min(jnp.where(rest == m2, row_f, float(N_EXPERTS)), axis=0, keepdims=True)
    picked = jnp.where((row_f == i1) | (row_f == i2), scores, 0.0)
    combt_ref[...] = picked / jnp.sum(picked, axis=0, keepdims=True)
    grp_ref[...] = grp


def _out_proj_call(y_ssd, y_da, y_ret, x2d, mod, w_out, ln_g, ln_b, router_w, router_b, seq_len):
    n = x2d.shape[0]
    tm = 512
    per_seq = max(seq_len // tm, 1)
    mod_idx = (lambda i: (i // per_seq, 0, 0)) if mod.shape[0] > 1 else (lambda i: (0, 0, 0))
    row = lambda w: pl.BlockSpec((tm, w), lambda i: (i, 0))
    col = lambda h: pl.BlockSpec((h, tm), lambda i: (0, i))
    full = lambda s: pl.BlockSpec(s, lambda i: (0,) * len(s))
    return pl.pallas_call(
        _out_proj_kernel,
        grid=(n // tm,),
        in_specs=[row(D_SSD), row(D_DA), row(D_RET), row(D_MODEL),
                  pl.BlockSpec((1, 8, D_MODEL), mod_idx),
                  full((D_MODEL, D_MODEL)), full((1, D_MODEL)), full((1, D_MODEL)),
                  full((N_EXPERTS, D_MODEL)), full((N_EXPERTS, 1))],
        out_specs=[row(D_MODEL), row(D_MODEL), col(N_EXPERTS), col(1)],
        out_shape=[jax.ShapeDtypeStruct((n, D_MODEL), F32),
                   jax.ShapeDtypeStruct((n, D_MODEL), BF16),
                   jax.ShapeDtypeStruct((N_EXPERTS, n), F32),
                   jax.ShapeDtypeStruct((1, n), jnp.int32)],
        compiler_params=_params(("arbitrary",)),
        name="out_proj_router",
    )(y_ssd, y_da, y_ret, x2d, mod, w_out, ln_g.reshape(1, D_MODEL), ln_b.reshape(1, D_MODEL),
      router_w.T, router_b.reshape(N_EXPERTS, 1))


MOE_T = 1024
MOE_TILE = 128
MOE_SLOTS = MOE_T + N_GROUPS * MOE_TILE
MOE_COL = 256


def _moe_kernel(xm_ref, combt_ref, grp_ref, wg_ref, wu_ref, wd_ref, x1_ref, mod_ref, g_ref, b_ref, o_ref,
                p_ref, xs_ref, cs_ref, y_ref, seg_ref):
    e = pl.program_id(1)
    n_tiles = MOE_SLOTS // MOE_TILE

    @pl.when(e == 0)
    def _():
        grp = grp_ref[...]
        onehot = lax.broadcasted_iota(jnp.int32, (N_GROUPS, MOE_T), 0) == grp
        onehot_f = jnp.where(onehot, 1.0, 0.0)
        bi = lax.broadcasted_iota(jnp.int32, (128, 128), 0)
        bj = lax.broadcasted_iota(jnp.int32, (128, 128), 1)
        strict_upper = jnp.where(bi < bj, 1.0, 0.0).astype(BF16)
        running = jnp.zeros((N_GROUPS, 1), F32)
        ranks = []
        for blk in range(MOE_T // 128):
            oh_b = onehot_f[:, 128 * blk:128 * (blk + 1)]
            ranks.append(_dot(oh_b.astype(BF16), strict_upper) + running)
            running = running + jnp.sum(oh_b, axis=1, keepdims=True)
        rank = jnp.concatenate(ranks, axis=1)
        tiles = lax.shift_right_logical(running.astype(jnp.int32) + (MOE_TILE - 1), 7)
        starts = [jnp.zeros((1, 1), jnp.int32)]
        for g in range(1, N_GROUPS):
            starts.append(starts[-1] + tiles[g - 1:g, :])
        used = starts[-1] + tiles[N_GROUPS - 1:, :]
        start = jnp.concatenate(starts, axis=0)
        pos = jnp.sum(jnp.where(onehot, (start * MOE_TILE).astype(F32) + rank, 0.0), axis=0, keepdims=True)
        pos = pos.astype(jnp.int32)
        for g in range(N_GROUPS):
            seg_ref[g] = jnp.sum(start[g:g + 1, :])
            seg_ref[N_GROUPS + g] = jnp.sum(tiles[g:g + 1, :])

        def fill(i, c):
            off = pl.multiple_of(i * MOE_TILE, MOE_TILE)
            slot = lax.broadcasted_iota(jnp.int32, (MOE_TILE, MOE_T), 0) + off
            p_ref[pl.ds(off, MOE_TILE), :] = jnp.where(slot == pos, 1.0, 0.0).astype(BF16)
            return c
        lax.fori_loop(0, n_tiles, fill, 0)

        p = p_ref[...]
        for cb in range(D_MODEL // MOE_COL):
            cols = slice(MOE_COL * cb, MOE_COL * (cb + 1))
            xs_ref[:, cols] = _dot(p, xm_ref[:, cols]).astype(BF16)
        ct = combt_ref[...]
        hi = ct.astype(BF16)
        r1 = ct - hi.astype(F32)
        mid = r1.astype(BF16)
        lo = (r1 - mid.astype(F32)).astype(BF16)
        pieces = jnp.concatenate([hi, mid, lo], axis=0)
        perm = lax.dot_general(pieces, p, (((1,), (1,)), ((), ())), preferred_element_type=F32)
        cs_ref[...] = (perm[:N_EXPERTS] + perm[N_EXPERTS:2 * N_EXPERTS] + perm[2 * N_EXPERTS:]).T

        def clear(i, c):
            off = pl.multiple_of(i * MOE_TILE, MOE_TILE)
            y_ref[pl.ds(off, MOE_TILE), :] = jnp.zeros((MOE_TILE, D_MODEL), F32)
            return c
        lax.fori_loop(jnp.sum(used), n_tiles, clear, 0)

    g_e = lax.shift_right_logical(e, 2)
    first_tile = seg_ref[g_e]
    group_tiles = seg_ref[N_GROUPS + g_e]
    first_in_group = (e & (EXPERTS_PER_GROUP - 1)) == 0
    lane = lax.broadcasted_iota(jnp.int32, (1, N_EXPERTS), 1)

    def expert_rows(tile, rows):
        off = pl.multiple_of(tile * MOE_TILE, MOE_TILE)
        xs = xs_ref[pl.ds(off, rows), :]
        h = _silu(_dot(xs, wg_ref[0, 0])) * _dot(xs, wu_ref[0, 0])
        y = _dot(h.astype(BF16), wd_ref[0, 0])
        w_e = jnp.sum(jnp.where(lane == e, cs_ref[pl.ds(off, rows), :], 0.0), axis=-1, keepdims=True)

        @pl.when(first_in_group)
        def _():
            y_ref[pl.ds(off, rows), :] = w_e * y

        @pl.when(jnp.logical_not(first_in_group))
        def _():
            y_ref[pl.ds(off, rows), :] += w_e * y

    def two_tiles(i, c):
        expert_rows(first_tile + 2 * i, 2 * MOE_TILE)
        return c
    lax.fori_loop(0, lax.shift_right_logical(group_tiles, 1), two_tiles, 0)

    @pl.when((group_tiles & 1) == 1)
    def _():
        expert_rows(first_tile + group_tiles - 1, MOE_TILE)

    @pl.when(e == N_EXPERTS - 1)
    def _():
        p = p_ref[...]
        for cb in range(D_MODEL // MOE_COL):
            cols = slice(MOE_COL * cb, MOE_COL * (cb + 1))
            o_ref[:, cols] = lax.dot_general(p, y_ref[:, cols].astype(BF16), (((0,), (0,)), ((), ())),
                                             preferred_element_type=F32)
        mod = mod_ref[0]
        o_ref[...] = _layer_norm(ALPHA * x1_ref[...] + mod[5:6] * o_ref[...], g_ref[...], b_ref[...])


def _moe_call(xm2, combt, grp, wg, wu, wd, layer, x1, mod, ln_g, ln_b, seq_len):
    n = xm2.shape[0]
    per_seq = max(seq_len // MOE_T, 1)
    mod_idx = (lambda i, e: (i // per_seq, 0, 0)) if mod.shape[0] > 1 else (lambda i, e: (0, 0, 0))
    row = lambda w: pl.BlockSpec((MOE_T, w), lambda i, e: (i, 0))
    col = lambda h: pl.BlockSpec((h, MOE_T), lambda i, e: (0, i))
    return pl.pallas_call(
        _moe_kernel,
        grid=(n // MOE_T, N_EXPERTS),
        in_specs=[row(D_MODEL), col(N_EXPERTS), col(1),
                  pl.BlockSpec((1, 1, D_MODEL, D_FF), lambda i, e: (layer, e, 0, 0)),
                  pl.BlockSpec((1, 1, D_MODEL, D_FF), lambda i, e: (layer, e, 0, 0)),
                  pl.BlockSpec((1, 1, D_FF, D_MODEL), lambda i, e: (layer, e, 0, 0)),
                  row(D_MODEL),
                  pl.BlockSpec((1, 8, D_MODEL), mod_idx),
                  pl.BlockSpec((1, D_MODEL), lambda i, e: (0, 0)),
                  pl.BlockSpec((1, D_MODEL), lambda i, e: (0, 0))],
        out_specs=row(D_MODEL),
        out_shape=jax.ShapeDtypeStruct((n, D_MODEL), F32),
        scratch_shapes=[pltpu.VMEM((MOE_SLOTS, MOE_T), BF16),
                        pltpu.VMEM((MOE_SLOTS, D_MODEL), BF16),
                        pltpu.VMEM((MOE_SLOTS, N_EXPERTS), F32),
                        pltpu.VMEM((MOE_SLOTS, D_MODEL), F32),
                        pltpu.SMEM((2 * N_GROUPS,), jnp.int32)],
        compiler_params=_params(("arbitrary", "arbitrary")),
        name="moe_grouped",
    )(xm2, combt, grp, wg, wu, wd, x1, mod, ln_g.reshape(1, D_MODEL), ln_b.reshape(1, D_MODEL))


def _pack_in_proj(w_in_l, rope):
    z, xbc, dtw, da, ret = jnp.split(w_in_l, (512, 1280, 1296, 2064), axis=1)
    wq, wk, wv = jnp.split(da, 3, axis=1)
    wrq, wrk, wrv, wrg = jnp.split(ret, 4, axis=1)
    nn_w = {"z": z, "xbc": xbc, "k": wk, "rq": wrq, "rv": wrv, "rg": wrg, "v": wv,
            "dt": jnp.pad(dtw, ((0, 0), (0, DT_COLS - dtw.shape[1])))}
    nt_w = {"q": wq.T, "v": wv.T, "rk": wrk.T, "dt": jnp.pad(dtw.T, ((0, DT_ROWS - dtw.shape[1]), (0, 0)))}
    cols, rows = _in_proj_layout(rope)
    return (jnp.concatenate([nn_w[name] for name in cols], axis=1).astype(BF16),
            jnp.concatenate([nt_w[name] for name in rows], axis=0).astype(BF16))


def _rope_tables(seq_len, dim, reps):
    nf = dim // 4
    inv = ROPE_BASE ** (-jnp.arange(nf, dtype=F32) / nf)
    t = jnp.arange(seq_len)
    r = (t // GRID_W).astype(F32)[:, None] * inv
    c = (t % GRID_W).astype(F32)[:, None] * inv
    ang = jnp.tile(jnp.concatenate([r, r, c, c], axis=-1), (1, reps))
    return jnp.cos(ang), jnp.sin(ang)


def _layer(x, mod, l, P, ctx, rope_tabs):
    bsz, seq_len, _ = x.shape
    n = bsz * seq_len
    x2d = x.reshape(n, D_MODEL)
    latent = ctx is not None
    wnn, wnt = P["in_proj"][l][1 if latent else 0]
    outs = _in_proj_call(x2d, mod, wnn, wnt, seq_len, rope_tabs)
    z, xbc, dt, dtt, qt, k, vt, rq, rkt, rv, rg = outs[:11]
    r3 = lambda a: a.reshape(bsz, seq_len, a.shape[-1])

    h_ssd0 = ctx[2] if latent else None
    h_ret0 = ctx[3] if latent else None
    ssd_out = _ssd_call(r3(xbc), r3(z), r3(dt), dtt, P["ssd_conv_w"][l], P["ssd_conv_b"][l],
                        P["ssd_dt_bias"][l], P["ssd_a_log"][l], P["ssd_d"][l], P["ssd_norm_w"][l], h_ssd0)
    ret_out = _ret_call(r3(rq), rkt, r3(rv), r3(rg), P["ret_decay"][l], P["ret_norm_w"][l], h_ret0)

    ks, vts = [r3(k)], [vt]
    if latent:
        y_ssd, y_ret = ssd_out, ret_out
        new = None
        ks.append(ctx[0].reshape(bsz, -1, D_DA).astype(BF16))
        vts.append(jnp.transpose(ctx[1].reshape(bsz, -1, D_DA), (0, 2, 1)).astype(BF16))
    else:
        y_ssd, hs = ssd_out
        y_ret, hr = ret_out
        v = outs[11]
        new = (k.reshape(bsz, seq_len, H_DA, DA_HEAD_DIM), v.reshape(bsz, seq_len, H_DA, DA_HEAD_DIM), hs, hr)
    lam_init = 0.8 - 0.6 * math.exp(-0.3 * l)
    y_da = _da_call(qt, ks, vts, P["da_lambda"][l], P["da_norm_w"][l], lam_init)

    x1, xm2, combt, grp = _out_proj_call(y_ssd.reshape(n, D_SSD), y_da.reshape(n, D_DA),
                                         y_ret.reshape(n, D_RET), x2d, mod, P["w_out16"][l],
                                         P["ln_mix_g"][l], P["ln_mix_b"][l], P["router_w"], P["router_b"],
                                         seq_len)
    x2 = _moe_call(xm2, combt, grp, P["wg16"], P["wu16"], P["wd16"], l, x1, mod,
                   P["ln_ffn_g"][l], P["ln_ffn_b"][l], seq_len)
    return x2.reshape(bsz, seq_len, D_MODEL), new


def kernel(x_prompt, x_sample, c, cache_da_k, cache_da_v, state_ssd, state_ret, c_ctx, w_ada, b_ada, w_in,
           ssd_conv_w, ssd_conv_b, ssd_dt_bias, ssd_a_log, ssd_d, ssd_norm_w, da_lambda, da_norm_w, ret_decay,
           ret_norm_w, w_out, ln_mix_g, ln_mix_b, router_w, router_b, moe_w_gate, moe_w_up, moe_w_down,
           ln_ffn_g, ln_ffn_b):
    dec_b = x_sample.shape[0]
    P = dict(ssd_conv_w=ssd_conv_w, ssd_conv_b=ssd_conv_b, ssd_dt_bias=ssd_dt_bias, ssd_a_log=ssd_a_log,
             ssd_d=ssd_d, ssd_norm_w=ssd_norm_w, da_lambda=da_lambda, da_norm_w=da_norm_w,
             ret_decay=ret_decay, ret_norm_w=ret_norm_w, ln_mix_g=ln_mix_g, ln_mix_b=ln_mix_b,
             router_w=router_w, router_b=router_b, ln_ffn_g=ln_ffn_g, ln_ffn_b=ln_ffn_b)
    P["in_proj"] = [(_pack_in_proj(w_in[l], False), _pack_in_proj(w_in[l], True)) for l in range(DEPTH)]
    P["w_out16"] = w_out.astype(BF16)
    P["wg16"] = moe_w_gate.astype(BF16)
    P["wu16"] = moe_w_up.astype(BF16)
    P["wd16"] = moe_w_down.astype(BF16)

    cond8 = jnp.concatenate([c_ctx[None, :], c, jnp.zeros((8 - 1 - dec_b, D_MODEL), F32)], axis=0)
    mod_all = _ada_call(cond8, w_ada, b_ada).reshape(DEPTH, 8, 6, D_MODEL)
    mod_all = jnp.pad(mod_all, ((0, 0), (0, 0), (0, 2), (0, 0)))

    y = x_prompt
    ks_, vs_, hs_, hr_ = [], [], [], []
    for l in range(DEPTH):
        y, (k_l, v_l, hs_l, hr_l) = _layer(y, mod_all[l, 0:1], l, P, None, None)
        ks_.append(k_l)
        vs_.append(v_l)
        hs_.append(hs_l)
        hr_.append(hr_l)
    def stack_layers(parts):
        bsz = parts[0].shape[0]
        flat = jnp.concatenate([p.reshape(bsz, 1, -1) for p in parts], axis=1)
        return flat.reshape((bsz, len(parts)) + parts[0].shape[1:])

    new_da_k = stack_layers(ks_)
    new_da_v = stack_layers(vs_)
    new_ssd = stack_layers(hs_)
    new_ret = stack_layers(hr_)

    seq_len = x_sample.shape[1]
    cos_da, sin_da = _rope_tables(seq_len, DA_HALF, D_DA // DA_HALF)
    cos_ret, sin_ret = _rope_tables(seq_len, RET_HEAD_DIM, H_RET)
    rope_tabs = (cos_da, sin_da, cos_ret, sin_ret, cos_da.T, sin_da.T, cos_ret.T, sin_ret.T)
    zl = x_sample
    for l in range(DEPTH):
        ctx = (cache_da_k[:, l], cache_da_v[:, l], state_ssd[:, l], state_ret[:, l])
        zl, _ = _layer(zl, mod_all[l, 1:1 + dec_b], l, P, ctx, rope_tabs)

    return (y, zl, new_da_k, new_da_v, new_ssd, new_ret)
```

```python
import functools
import math

import jax
import jax.numpy as jnp
import numpy as np
from jax import lax
from jax.experimental import pallas as pl
from jax.experimental.pallas import tpu as pltpu

D_MODEL = 1024
DEPTH = 2
GRID_W = 64
CHUNK = 128
H_SSD = 8
SSD_HEAD_DIM = 64
D_SSD = H_SSD * SSD_HEAD_DIM
SSD_GROUPS = 2
D_STATE = 64
CONV_W = 5
D_XBC = D_SSD + 2 * SSD_GROUPS * D_STATE
H_DA = 4
DA_HALF = 32
DA_HEAD_DIM = 2 * DA_HALF
D_DA = H_DA * DA_HEAD_DIM
H_RET = 4
RET_HEAD_DIM = 64
D_RET = H_RET * RET_HEAD_DIM
N_EXPERTS = 16
EXPERTS_PER_GROUP = 4
N_GROUPS = N_EXPERTS // EXPERTS_PER_GROUP
D_FF = 512
ROPE_BASE = 10000.0
EPS = 1e-6
ALPHA = (2 * DEPTH) ** 0.25

F32 = jnp.float32
BF16 = jnp.bfloat16
HIGHEST = lax.Precision.HIGHEST
NEG_INF = float("-inf")
LOG2_E = 1.4426950408889634

V7X_VMEM_LIMIT_BYTES = 56 * 1024 * 1024
CONV_HALO = 8

DT_COLS = 128
DT_ROWS = 32


def _in_proj_layout(rope):
    nn = [("z", D_SSD), ("xbc", D_XBC), ("k", D_DA), ("rq", D_RET), ("rv", D_RET), ("rg", D_RET), ("dt", DT_COLS)]
    nt = [("q", D_DA), ("v", D_DA), ("rk", D_RET), ("dt", DT_ROWS)]
    if not rope:
        nn += [("v", D_DA)]

    def spans(parts):
        out, at = {}, 0
        for name, width in parts:
            out[name] = (at, at + width)
            at += width
        return out
    return spans(nn), spans(nt)


def _silu(x):
    return x * (1.0 / (1.0 + jnp.exp(-x)))


def _softplus(x):
    return jnp.maximum(x, 0.0) + jnp.log1p(jnp.exp(-jnp.abs(x)))


def _dot(a, b, precision=None):
    return jnp.dot(a, b, preferred_element_type=F32, precision=precision)


def _layer_norm(t, g, b):
    mu = jnp.mean(t, axis=-1, keepdims=True)
    tc = t - mu
    var = jnp.mean(tc * tc, axis=-1, keepdims=True)
    return tc * lax.rsqrt(var + EPS) * g + b


def _params(sem):
    return pltpu.CompilerParams(dimension_semantics=sem, vmem_limit_bytes=V7X_VMEM_LIMIT_BYTES)


def _ada_kernel(cond_ref, w_ref, b_ref, o_ref):
    o_ref[0] = _dot(_silu(cond_ref[...]), w_ref[0], HIGHEST) + b_ref[0]


def _ada_call(cond8, w_ada, b_ada):
    tn = 1536
    nb = (6 * D_MODEL) // tn
    return pl.pallas_call(
        _ada_kernel,
        grid=(DEPTH, nb),
        in_specs=[pl.BlockSpec((8, D_MODEL), lambda l, n: (0, 0)),
                  pl.BlockSpec((1, D_MODEL, tn), lambda l, n: (l, 0, n)),
                  pl.BlockSpec((1, 1, tn), lambda l, n: (l, 0, n))],
        out_specs=pl.BlockSpec((1, 8, tn), lambda l, n: (l, 0, n)),
        out_shape=jax.ShapeDtypeStruct((DEPTH, 8, 6 * D_MODEL), F32),
        compiler_params=_params(("arbitrary", "arbitrary")),
        name="ada_mod",
    )(cond8, w_ada, b_ada.reshape(DEPTH, 1, 6 * D_MODEL))


def _rot_rows(x, dim):
    nf, m = dim // 4, dim // 2
    parts = []
    for b in range(0, x.shape[0], dim):
        parts += [-x[b + nf:b + m], x[b:b + nf], -x[b + m + nf:b + dim], x[b + m:b + m + nf]]
    return jnp.concatenate(parts, axis=0)


def _rot_lanes(x, dim):
    nf, n = dim // 4, x.shape[1]
    lane = lax.broadcasted_iota(jnp.int32, (1, n), 1)
    ahead = pltpu.roll(x, n - nf, axis=1)
    behind = pltpu.roll(x, nf, axis=1)
    return jnp.where((lane & (2 * nf - 1)) < nf, -ahead, behind)


def _in_proj_kernel(rope, *refs):
    if rope:
        (x_ref, mod_ref, wnn_ref, wnt_ref, cd_ref, sd_ref, cr_ref, sr_ref,
         cdt_ref, sdt_ref, crt_ref, srt_ref,
         z_ref, xbc_ref, dt_ref, dtt_ref, qt_ref, k_ref, vt_ref, rq_ref, rkt_ref, rv_ref, rg_ref) = refs
    else:
        (x_ref, mod_ref, wnn_ref, wnt_ref,
         z_ref, xbc_ref, dt_ref, dtt_ref, qt_ref, k_ref, vt_ref, rq_ref, rkt_ref, rv_ref, rg_ref,
         v_ref) = refs
    cols, rows = _in_proj_layout(rope)
    mod = mod_ref[0]
    xm = (x_ref[...] * (1.0 + mod[1:2]) + mod[0:1]).astype(BF16)

    def nn(name):
        a, b = cols[name]
        return _dot(xm, wnn_ref[:, a:b])

    def nt(name):
        a, b = rows[name]
        return lax.dot_general(wnt_ref[a:b, :], xm, (((1,), (1,)), ((), ())),
                               preferred_element_type=F32)

    z_ref[...] = nn("z")
    xbc_ref[...] = nn("xbc")
    dt_ref[...] = nn("dt")[:, :2 * H_SSD]
    dtt_ref[0] = nt("dt")[:2 * H_SSD, :]
    vt_ref[0] = nt("v").astype(BF16)
    rv_ref[...] = nn("rv")
    rg_ref[...] = nn("rg")
    qt = nt("q")
    k = nn("k")
    rq = nn("rq")
    rkt = nt("rk")
    if rope:
        qt = qt * cdt_ref[...] + _rot_rows(qt, DA_HALF) * sdt_ref[...]
        k = k * cd_ref[...] + _rot_lanes(k, DA_HALF) * sd_ref[...]
        rq = rq * cr_ref[...] + _rot_lanes(rq, RET_HEAD_DIM) * sr_ref[...]
        rkt = rkt * crt_ref[...] + _rot_rows(rkt, RET_HEAD_DIM) * srt_ref[...]
    else:
        v_ref[...] = nn("v")
    qt_ref[0] = (qt * (DA_HALF ** -0.5 * LOG2_E)).astype(BF16)
    k_ref[...] = k.astype(k_ref.dtype)
    rq_ref[...] = (rq * (RET_HEAD_DIM ** -0.5)).astype(BF16)
    rkt_ref[0] = rkt.astype(BF16)


def _in_proj_call(x2d, mod, wnn, wnt, seq_len, rope_tabs):
    n = x2d.shape[0]
    tm = min(512, seq_len)
    nblk = n // tm
    per_seq = seq_len // tm
    n_mod = mod.shape[0]
    rope = rope_tabs is not None
    mod_idx = (lambda i: (i // per_seq, 0, 0)) if n_mod > 1 else (lambda i: (0, 0, 0))
    bsz = n // seq_len
    row = lambda w: pl.BlockSpec((tm, w), lambda i: (i, 0))
    col = lambda h: pl.BlockSpec((1, h, tm), lambda i: (i // per_seq, 0, i % per_seq))
    in_specs = [row(D_MODEL),
                pl.BlockSpec((1, 8, D_MODEL), mod_idx),
                pl.BlockSpec(wnn.shape, lambda i: (0, 0)),
                pl.BlockSpec(wnt.shape, lambda i: (0, 0))]
    args = [x2d, mod, wnn, wnt]
    if rope:
        in_specs += [pl.BlockSpec((tm, 256), lambda i: (i % per_seq, 0))] * 4
        in_specs += [pl.BlockSpec((256, tm), lambda i: (0, i % per_seq))] * 4
        args += list(rope_tabs)
    sds = jax.ShapeDtypeStruct
    out_shape = [sds((n, D_SSD), F32), sds((n, D_XBC), F32), sds((n, 2 * H_SSD), F32),
                 sds((bsz, 2 * H_SSD, seq_len), F32), sds((bsz, D_DA, seq_len), BF16),
                 sds((n, D_DA), BF16 if rope else F32), sds((bsz, D_DA, seq_len), BF16),
                 sds((n, D_RET), BF16), sds((bsz, D_RET, seq_len), BF16),
                 sds((n, D_RET), F32), sds((n, D_RET), F32)]
    out_specs = [row(D_SSD), row(D_XBC), row(2 * H_SSD), col(2 * H_SSD), col(D_DA), row(D_DA),
                 col(D_DA), row(D_RET), col(D_RET), row(D_RET), row(D_RET)]
    if not rope:
        out_shape.append(sds((n, D_DA), F32))
        out_specs.append(row(D_DA))
    return pl.pallas_call(
        functools.partial(_in_proj_kernel, rope),
        grid=(nblk,),
        in_specs=in_specs,
        out_specs=out_specs,
        out_shape=out_shape,
        compiler_params=_params(("arbitrary",)),
        name="in_proj_rope" if rope else "in_proj",
    )(*args)


SEQS_PER_STEP = 4


def _seqs_per_step(bsz):
    assert bsz % SEQS_PER_STEP == 0
    return SEQS_PER_STEP


SSD_SEQ_REFS = (0, 1, 2, 3, 4, 5)
RET_SEQ_REFS = (0, 1, 2, 3)


def _per_sequence(seq_kernel, n_in, seq_in, n_scratch, nc, nb, has_h0, *refs):
    ins = list(refs[:n_in])
    outs = list(refs[n_in:len(refs) - n_scratch])
    scratch = list(refs[len(refs) - n_scratch:])
    live = []
    for s in range(nb):
        one = lambda r, s=s: r.at[pl.ds(s, 1)]
        seq_ins = [one(r) if i in seq_in else r for i, r in enumerate(ins)]
        live.append(seq_kernel(nc, has_h0, *seq_ins, *[one(r) for r in outs], *[r.at[s] for r in scratch]))
    while live:
        still = []
        for gen in live:
            try:
                next(gen)
                still.append(gen)
            except StopIteration:
                pass
        live = still


CONV_SHIFT_TAPS = tuple(k for k in range(CONV_W) if k != CONV_W // 2)
CONV_EXT_ROWS = CHUNK + 2 * CONV_HALO
CONV_SHIFT_SHAPE = (len(CONV_SHIFT_TAPS) * CHUNK, 2 * CONV_EXT_ROWS)


def _conv_shift_matrix():
    s = np.zeros(CONV_SHIFT_SHAPE, np.float32)
    t = np.arange(CHUNK)
    for idx, k in enumerate(CONV_SHIFT_TAPS):
        src = CONV_HALO + t + k - CONV_W // 2
        s[idx * CHUNK + t, src] = 1.0
        s[idx * CHUNK + t, CONV_EXT_ROWS + src] = 1.0
    return jnp.asarray(s, dtype=BF16)


def _ssd_seq(nc, has_h0, *refs):
    refs = list(refs)
    (xm_ref, xp_ref, xn_ref, z_ref, dt_ref, dtt_ref, cw_ref, cb_ref, dtb_c_ref, dtb_r_ref,
     alog_c_ref, alog_r_ref, dskip_ref, nw_ref, shift_ref) = refs[:15]
    refs = refs[15:]
    h0_ref = refs.pop(0) if has_h0 else None
    y_ref = refs.pop(0)
    hs_ref = None if has_h0 else refs.pop(0)
    hp_ref, yf_ref = refs

    d = pl.program_id(1)
    j = pl.program_id(2)
    fwd = d == 0
    cj = jnp.where(fwd, j, nc - 1 - j)

    @pl.when(j == 0)
    def _():
        for p in range(H_SSD // 2):
            if has_h0:
                g = p // 2
                both = jnp.concatenate([h0_ref[0, pl.ds(d, 1), 2 * p][0],
                                        h0_ref[0, pl.ds(d, 1), 2 * p + 1][0]], axis=1)
                zero = jnp.zeros((D_STATE, 128), F32)
                hp_ref[p] = jnp.concatenate([both, zero] if g == 0 else [zero, both], axis=0)
            else:
                hp_ref[p] = jnp.zeros((128, 128), F32)
    yield

    x_c = xm_ref[0]
    ext = jnp.concatenate([jnp.where(cj > 0, xp_ref[0], 0.0), x_c, jnp.where(cj < nc - 1, xn_ref[0], 0.0)],
                          axis=0)
    hi = ext.astype(BF16)
    lo = (ext - hi.astype(F32)).astype(BF16)
    shifted = _dot(shift_ref[...], jnp.concatenate([hi, lo], axis=0))
    acc = cb_ref[...] + cw_ref[CONV_W // 2:CONV_W // 2 + 1, :] * x_c
    for idx, k in enumerate(CONV_SHIFT_TAPS):
        acc = acc + cw_ref[k:k + 1, :] * shifted[CHUNK * idx:CHUNK * (idx + 1), :]
    act = _silu(acc)
    xs = act[:, :D_SSD]
    bm = act[:, D_SSD:D_SSD + 128]
    cm = act[:, D_SSD + 128:]
    yield

    dt_c_all = _softplus(dt_ref[0] + dtb_c_ref[...])
    dt_r_all = _softplus(dtt_ref[0] + dtb_r_ref[...])
    a_c_all = -jnp.exp(alog_c_ref[...])
    a_r_all = -jnp.exp(alog_r_ref[...])
    dt_c = jnp.where(fwd, dt_c_all[:, :H_SSD], dt_c_all[:, H_SSD:])
    dt_r = jnp.where(fwd, dt_r_all[:H_SSD], dt_r_all[H_SSD:])
    la_c = dt_c * jnp.where(fwd, a_c_all[:, :H_SSD], a_c_all[:, H_SSD:])
    la_r = dt_r * jnp.where(fwd, a_r_all[:H_SSD], a_r_all[H_SSD:])

    ii = lax.broadcasted_iota(jnp.int32, (CHUNK, CHUNK), 0)
    jj = lax.broadcasted_iota(jnp.int32, (CHUNK, CHUNK), 1)
    valid = jnp.where(fwd, ii - jj, jj - ii) >= 0
    valid_t = jnp.where(fwd, jj - ii, ii - jj) >= 0
    cum_c = _dot(valid.astype(F32), la_c, HIGHEST)
    cum_r = _dot(la_r, valid_t.astype(F32), HIGHEST)
    tot_r = jnp.where(fwd, cum_r[:, CHUNK - 1:], cum_r[:, :1])
    e_tot = jnp.exp(tot_r)
    wk_r = dt_r * jnp.exp(tot_r - cum_r)
    e_cum_c = jnp.exp(cum_c)
    cum2_c = cum_c * LOG2_E
    cum2_r = (cum_r - jnp.log(dt_r)) * LOG2_E

    yield
    bm_t = bm.T
    bm_t16 = bm_t.astype(BF16)
    lane = lax.broadcasted_iota(jnp.int32, (1, 128), 1)
    low_lanes = lane < 64
    feat_row = lax.broadcasted_iota(jnp.int32, (128, 1), 0)

    ys = []
    for g in range(SSD_GROUPS):
        in_group_lane = (lane >= 64 * g) & (lane < 64 * (g + 1))
        cg = jnp.where(in_group_lane, cm, 0.0)
        gram = _dot(cg.astype(BF16), bm_t16)
        in_group_row = (feat_row >= 64 * g) & (feat_row < 64 * (g + 1))
        for p in range(2 * g, 2 * g + 2):
            xs_p = xs[:, 128 * p:128 * (p + 1)].astype(BF16)
            h_prev = hp_ref[p]
            h_prev16 = h_prev.astype(BF16)
            y_h, s_h = [], []
            for hh in (2 * p, 2 * p + 1):
                decay_dt = jnp.exp2(jnp.where(valid, cum2_c[:, hh:hh + 1] - cum2_r[hh:hh + 1, :], NEG_INF))
                sc = gram * decay_dt
                y_intra = _dot(sc.astype(BF16), xs_p)
                y_inter = _dot((cg * e_cum_c[:, hh:hh + 1]).astype(BF16), h_prev16)
                y_h.append(y_intra + y_inter)
                s_h.append(_dot((bm_t * wk_r[hh:hh + 1, :]).astype(BF16), xs_p))
            ys.append(jnp.where(low_lanes, y_h[0], y_h[1]))
            s_pair = jnp.where(low_lanes, s_h[0], s_h[1])
            e_pair = jnp.where(low_lanes, e_tot[2 * p:2 * p + 1, :], e_tot[2 * p + 1:2 * p + 2, :])
            hp_ref[p] = jnp.where(in_group_row, e_pair * h_prev + s_pair, 0.0)
            yield
    y_dir = jnp.concatenate(ys, axis=1)

    if hs_ref is not None:
        @pl.when(j == nc - 1)
        def _():
            for dd in range(2):
                @pl.when(d == dd)
                def _():
                    for p in range(H_SSD // 2):
                        g = p // 2
                        hn = hp_ref[p]
                        hs_ref[0, dd, 2 * p] = hn[64 * g:64 * (g + 1), :64]
                        hs_ref[0, dd, 2 * p + 1] = hn[64 * g:64 * (g + 1), 64:]

    @pl.when(fwd)
    def _():
        yf_ref[cj] = y_dir.astype(yf_ref.dtype)

    @pl.when(d == 1)
    def _():
        yt = yf_ref[cj].astype(F32) + y_dir + dskip_ref[...] * xs
        gated = yt * _silu(z_ref[0])
        ms = jnp.mean(gated * gated, axis=-1, keepdims=True)
        y_ref[0] = gated * lax.rsqrt(ms + EPS) * nw_ref[...]


def _ssd_call(xbc, z, dt, dtt, conv_w, conv_b, dt_bias, a_log, d_skip, norm_w, h0):
    bsz, seq_len, _ = xbc.shape
    nc = seq_len // CHUNK
    has_h0 = h0 is not None
    hb = CHUNK // CONV_HALO
    last_hb = seq_len // CONV_HALO - 1

    def chunk_of(d, j):
        return jnp.where(d == 0, j, nc - 1 - j)

    def full(shape):
        return pl.BlockSpec(shape, lambda b, d, j: (0,) * len(shape))

    nb = _seqs_per_step(bsz)
    in_specs = [
        pl.BlockSpec((nb, CHUNK, D_XBC), lambda b, d, j: (b, chunk_of(d, j), 0)),
        pl.BlockSpec((nb, CONV_HALO, D_XBC), lambda b, d, j: (b, jnp.maximum(chunk_of(d, j) * hb - 1, 0), 0)),
        pl.BlockSpec((nb, CONV_HALO, D_XBC),
                     lambda b, d, j: (b, jnp.minimum((chunk_of(d, j) + 1) * hb, last_hb), 0)),
        pl.BlockSpec((nb, CHUNK, D_SSD), lambda b, d, j: (b, chunk_of(d, j), 0)),
        pl.BlockSpec((nb, CHUNK, 2 * H_SSD), lambda b, d, j: (b, chunk_of(d, j), 0)),
        pl.BlockSpec((nb, 2 * H_SSD, CHUNK), lambda b, d, j: (b, 0, chunk_of(d, j))),
        full((CONV_W, D_XBC)), full((1, D_XBC)), full((1, 2 * H_SSD)), full((2 * H_SSD, 1)),
        full((1, 2 * H_SSD)), full((2 * H_SSD, 1)), full((1, D_SSD)), full((1, D_SSD)),
        full(CONV_SHIFT_SHAPE),
    ]
    args = [xbc, xbc, xbc, z, dt, dtt, conv_w, conv_b.reshape(1, D_XBC),
            dt_bias.reshape(1, 2 * H_SSD), dt_bias.reshape(2 * H_SSD, 1),
            a_log.reshape(1, 2 * H_SSD), a_log.reshape(2 * H_SSD, 1),
            jnp.repeat(d_skip, SSD_HEAD_DIM).reshape(1, D_SSD), norm_w.reshape(1, D_SSD),
            _conv_shift_matrix()]
    y_spec = pl.BlockSpec((nb, CHUNK, D_SSD), lambda b, d, j: (b, jnp.where(d == 0, nc - 1, nc - 1 - j), 0))
    y_shape = jax.ShapeDtypeStruct((bsz, seq_len, D_SSD), F32)
    state_block = (nb, 2, H_SSD, D_STATE, SSD_HEAD_DIM)
    state_spec = pl.BlockSpec(state_block, lambda b, d, j: (b, 0, 0, 0, 0))
    seq_in = SSD_SEQ_REFS
    if has_h0:
        seq_in = seq_in + (len(in_specs),)
        in_specs.append(state_spec)
        args.append(h0)
        out_specs, out_shape = y_spec, y_shape
    else:
        out_specs = [y_spec, state_spec]
        out_shape = [y_shape, jax.ShapeDtypeStruct((bsz,) + state_block[1:], F32)]
    return pl.pallas_call(
        functools.partial(_per_sequence, _ssd_seq, len(in_specs), seq_in, 2, nc, nb, has_h0),
        grid=(bsz // nb, 2, nc),
        in_specs=in_specs,
        out_specs=out_specs,
        out_shape=out_shape,
        scratch_shapes=[pltpu.VMEM((nb, H_SSD // 2, 128, 128), F32),
                        pltpu.VMEM((nb, nc, CHUNK, D_SSD), BF16)],
        compiler_params=_params(("arbitrary", "arbitrary", "arbitrary")),
        name="ssd_latent" if has_h0 else "ssd_ctx",
    )(*args)


def _ret_seq(nc, has_h0, *refs):
    refs = list(refs)
    q_ref, kt_ref, v_ref, g_ref, dec_ref, nw_ref = refs[:6]
    refs = refs[6:]
    h0_ref = refs.pop(0) if has_h0 else None
    y_ref = refs.pop(0)
    hs_ref = None if has_h0 else refs.pop(0)
    hr_ref, yf_ref = refs

    d = pl.program_id(1)
    j = pl.program_id(2)
    fwd = d == 0
    cj = jnp.where(fwd, j, nc - 1 - j)

    @pl.when(j == 0)
    def _():
        for p in range(H_RET // 2):
            if has_h0:
                zero = jnp.zeros((RET_HEAD_DIM, RET_HEAD_DIM), F32)
                top = jnp.concatenate([h0_ref[0, pl.ds(d, 1), 2 * p][0], zero], axis=1)
                bot = jnp.concatenate([zero, h0_ref[0, pl.ds(d, 1), 2 * p + 1][0]], axis=1)
                hr_ref[p] = jnp.concatenate([top, bot], axis=0)
            else:
                hr_ref[p] = jnp.zeros((128, 128), F32)
    yield

    dec = dec_ref[...]
    lg_all = -_softplus(-dec)
    lg = jnp.where(fwd, lg_all[0:1], lg_all[1:2])

    ii = lax.broadcasted_iota(jnp.int32, (CHUNK, CHUNK), 0)
    jj = lax.broadcasted_iota(jnp.int32, (CHUNK, CHUNK), 1)
    dist = jnp.where(fwd, ii - jj, jj - ii)
    valid = dist >= 0
    dist_f = dist.astype(F32)
    qpos = lax.broadcasted_iota(jnp.int32, (CHUNK, 1), 0)
    kpos = lax.broadcasted_iota(jnp.int32, (1, CHUNK), 1)
    n_q = jnp.where(fwd, qpos + 1, CHUNK - qpos).astype(F32)
    n_k = jnp.where(fwd, CHUNK - 1 - kpos, kpos).astype(F32)
    lane = lax.broadcasted_iota(jnp.int32, (1, 128), 1)
    low_lanes = lane < 64
    low_rows = lax.broadcasted_iota(jnp.int32, (128, 1), 0) < 64
    block_diag = low_rows == low_lanes

    ys = []
    for p in range(H_RET // 2):
        q_p = q_ref[0, :, 128 * p:128 * (p + 1)].astype(F32)
        kt_p = kt_ref[0, 128 * p:128 * (p + 1), :]
        v_p = v_ref[0, :, 128 * p:128 * (p + 1)].astype(BF16)
        h_prev = hr_ref[p]
        lg_a = lg[:, 2 * p:2 * p + 1]
        lg_b = lg[:, 2 * p + 1:2 * p + 2]
        y_h = []
        for lg_h, in_head in ((lg_a, low_lanes), (lg_b, jnp.logical_not(low_lanes))):
            s = _dot(jnp.where(in_head, q_p, 0.0).astype(BF16), kt_p)
            decay = jnp.exp(jnp.where(valid, dist_f * lg_h, NEG_INF))
            y_h.append(_dot((s * decay).astype(BF16), v_p))
        e_q = jnp.where(low_lanes, jnp.exp(n_q * lg_a), jnp.exp(n_q * lg_b))
        y_inter = _dot((q_p * e_q).astype(BF16), h_prev.astype(BF16))
        w_k = jnp.where(low_rows, jnp.exp(n_k * lg_a), jnp.exp(n_k * lg_b))
        s_new = _dot((kt_p.astype(F32) * w_k).astype(BF16), v_p)
        e_tot = jnp.where(low_lanes, jnp.exp(CHUNK * lg_a), jnp.exp(CHUNK * lg_b))
        hr_ref[p] = e_tot * h_prev + jnp.where(block_diag, s_new, 0.0)
        ys.append(jnp.where(low_lanes, y_h[0], y_h[1]) + y_inter)
        yield
    y_dir = jnp.concatenate(ys, axis=1)

    if hs_ref is not None:
        @pl.when(j == nc - 1)
        def _():
            for dd in range(2):
                @pl.when(d == dd)
                def _():
                    for p in range(H_RET // 2):
                        hn = hr_ref[p]
                        hs_ref[0, dd, 2 * p] = hn[:64, :64]
                        hs_ref[0, dd, 2 * p + 1] = hn[64:, 64:]

    @pl.when(fwd)
    def _():
        yf_ref[cj] = y_dir.astype(yf_ref.dtype)

    @pl.when(d == 1)
    def _():
        yt = yf_ref[cj].astype(F32) + y_dir
        lane4 = lax.shift_right_logical(lax.broadcasted_iota(jnp.int32, (1, D_RET), 1), 6)
        mu = jnp.zeros_like(yt)
        for h in range(H_RET):
            m_h = jnp.sum(jnp.where(lane4 == h, yt, 0.0), axis=-1, keepdims=True) * (1.0 / RET_HEAD_DIM)
            mu = jnp.where(lane4 == h, m_h, mu)
        yc = yt - mu
        sq = yc * yc
        rs = jnp.zeros_like(yt)
        for h in range(H_RET):
            v_h = jnp.sum(jnp.where(lane4 == h, sq, 0.0), axis=-1, keepdims=True) * (1.0 / RET_HEAD_DIM)
            rs = jnp.where(lane4 == h, lax.rsqrt(v_h + EPS), rs)
        y_ref[0] = _silu(g_ref[0]) * (yc * rs * nw_ref[...])


def _ret_call(rq, rkt, rv, rg, ret_decay, norm_w, h0):
    bsz, seq_len, _ = rq.shape
    nc = seq_len // CHUNK
    has_h0 = h0 is not None

    def chunk_of(d, j):
        return jnp.where(d == 0, j, nc - 1 - j)

    nb = _seqs_per_step(bsz)
    tok = pl.BlockSpec((nb, CHUNK, D_RET), lambda b, d, j: (b, chunk_of(d, j), 0))
    in_specs = [tok,
                pl.BlockSpec((nb, D_RET, CHUNK), lambda b, d, j: (b, 0, chunk_of(d, j))),
                tok, tok,
                pl.BlockSpec((2, H_RET), lambda b, d, j: (0, 0)),
                pl.BlockSpec((1, D_RET), lambda b, d, j: (0, 0))]
    args = [rq, rkt, rv, rg, ret_decay, jnp.tile(norm_w, H_RET).reshape(1, D_RET)]
    y_spec = pl.BlockSpec((nb, CHUNK, D_RET), lambda b, d, j: (b, jnp.where(d == 0, nc - 1, nc - 1 - j), 0))
    y_shape = jax.ShapeDtypeStruct((bsz, seq_len, D_RET), F32)
    state_block = (nb, 2, H_RET, RET_HEAD_DIM, RET_HEAD_DIM)
    state_spec = pl.BlockSpec(state_block, lambda b, d, j: (b, 0, 0, 0, 0))
    seq_in = RET_SEQ_REFS
    if has_h0:
        seq_in = seq_in + (len(in_specs),)
        in_specs.append(state_spec)
        args.append(h0)
        out_specs, out_shape = y_spec, y_shape
    else:
        out_specs = [y_spec, state_spec]
        out_shape = [y_shape, jax.ShapeDtypeStruct((bsz,) + state_block[1:], F32)]
    return pl.pallas_call(
        functools.partial(_per_sequence, _ret_seq, len(in_specs), seq_in, 2, nc, nb, has_h0),
        grid=(bsz // nb, 2, nc),
        in_specs=in_specs,
        out_specs=out_specs,
        out_shape=out_shape,
        scratch_shapes=[pltpu.VMEM((nb, H_RET // 2, 128, 128), F32),
                        pltpu.VMEM((nb, nc, CHUNK, D_RET), BF16)],
        compiler_params=_params(("arbitrary", "arbitrary", "arbitrary")),
        name="ret_latent" if has_h0 else "ret_ctx",
    )(*args)


DA_QUERY_BLOCK = 512
DA_KEY_CHUNK = 512
DA_SUM_ROWS = 16


def _da_kernel(lam_init, seg_lens, *refs):
    n_seg = len(seg_lens)
    qt_ref = refs[0]
    k_refs = refs[1:1 + n_seg]
    vt_refs = refs[1 + n_seg:1 + 2 * n_seg]
    lamp_ref, nw_ref, o_ref, s_scr, p_scr, va_scr = refs[1 + 2 * n_seg:]
    tq = qt_ref.shape[2]
    lk = sum(seg_lens)
    va_rows = DA_HEAD_DIM + DA_SUM_ROWS

    @pl.when(pl.program_id(1) == 0)
    def _():
        for h in range(H_DA):
            col = 0
            for seg in range(n_seg):
                keys = slice(col, col + seg_lens[seg])
                va_scr[h, 0:DA_HEAD_DIM, keys] = vt_refs[seg][0, DA_HEAD_DIM * h:DA_HEAD_DIM * (h + 1), :]
                va_scr[h, DA_HEAD_DIM:va_rows, keys] = jnp.ones((DA_SUM_ROWS, seg_lens[seg]), BF16)
                col += seg_lens[seg]

    lp = lamp_ref[...]
    s1 = jnp.sum(lp[0:1] * lp[1:2], axis=-1, keepdims=True)
    s2 = jnp.sum(lp[2:3] * lp[3:4], axis=-1, keepdims=True)
    lam = jnp.exp(s1) - jnp.exp(s2) + lam_init
    qt = qt_ref[0]
    half_of_row = lax.shift_right_logical(lax.broadcasted_iota(jnp.int32, (D_DA, 1), 0), 5)
    chunks, col = [], 0
    for seg in range(n_seg):
        size = min(DA_KEY_CHUNK, seg_lens[seg])
        for start in range(0, seg_lens[seg], size):
            chunks.append((seg, start, col, size))
            col += size
    n_soft = 2 * H_DA
    q_half, m8 = {}, {}

    def scores(j, ci):
        seg, start, col, size = chunks[ci]
        if j not in q_half:
            q_half[j] = jnp.where(half_of_row == j, qt, jnp.zeros_like(qt))
        k_c = k_refs[seg][0, start:start + size, :].astype(BF16)
        s = _dot(k_c, q_half[j])
        s_scr[j % 2, col:col + size, :] = s
        for r in range(size // 8):
            part = s[8 * r:8 * (r + 1), :]
            m8[j] = part if j not in m8 else jnp.maximum(m8[j], part)

    for ci in range(len(chunks)):
        scores(0, ci)
    o_heads, pv = [], []
    for j in range(n_soft):
        m = jnp.max(m8[j], axis=0, keepdims=True)
        for ci in range(len(chunks)):
            if j + 1 < n_soft:
                scores(j + 1, ci)
            keys = slice(chunks[ci][2], chunks[ci][2] + chunks[ci][3])
            p_scr[j % 2, keys, :] = jnp.exp2(s_scr[j % 2, keys, :] - m).astype(BF16)
        pv.append(_dot(va_scr[j // 2], p_scr[j % 2]))
        if j % 2 == 1:
            inv0 = 1.0 / pv[0][DA_HEAD_DIM:DA_HEAD_DIM + 1, :]
            inv1 = 1.0 / pv[1][DA_HEAD_DIM:DA_HEAD_DIM + 1, :]
            o_h = pv[0][:DA_HEAD_DIM, :] * inv0 - pv[1][:DA_HEAD_DIM, :] * (lam * inv1)
            ms = jnp.mean(o_h * o_h, axis=0, keepdims=True)
            o_heads.append(o_h * lax.rsqrt(ms + EPS) * nw_ref[...] * (1.0 - lam_init))
            pv = []
    o_ref[0] = jnp.concatenate(o_heads, axis=0).T


def _da_call(qt, ks, vts, lam_params, norm_w, lam_init):
    bsz, _, lq = qt.shape
    seg_lens = tuple(k.shape[1] for k in ks)
    lk = sum(seg_lens)
    tq = min(DA_QUERY_BLOCK, lq)
    in_specs = [pl.BlockSpec((1, D_DA, tq), lambda b, i: (b, 0, i))]
    in_specs += [pl.BlockSpec((1, n, D_DA), lambda b, i: (b, 0, 0)) for n in seg_lens]
    in_specs += [pl.BlockSpec((1, D_DA, n), lambda b, i: (b, 0, 0)) for n in seg_lens]
    in_specs += [pl.BlockSpec((4, DA_HALF), lambda b, i: (0, 0)),
                 pl.BlockSpec((DA_HEAD_DIM, 1), lambda b, i: (0, 0))]
    return pl.pallas_call(
        functools.partial(_da_kernel, lam_init, seg_lens),
        grid=(bsz, lq // tq),
        in_specs=in_specs,
        out_specs=pl.BlockSpec((1, tq, D_DA), lambda b, i: (b, i, 0)),
        out_shape=jax.ShapeDtypeStruct((bsz, lq, D_DA), F32),
        scratch_shapes=[pltpu.VMEM((2, lk, tq), F32),
                        pltpu.VMEM((2, lk, tq), BF16),
                        pltpu.VMEM((H_DA, DA_HEAD_DIM + DA_SUM_ROWS, lk), BF16)],
        compiler_params=_params(("arbitrary", "arbitrary")),
        name="diff_attn",
    )(qt, *ks, *vts, lam_params, norm_w.reshape(DA_HEAD_DIM, 1))


def _out_proj_kernel(ys_ref, yd_ref, yr_ref, x_ref, mod_ref, w_ref, g_ref, b_ref, rwt_ref, rb_ref,
                     x1_ref, xm2_ref, combt_ref, grp_ref):
    mix = (_dot(ys_ref[...].astype(BF16), w_ref[0:D_SSD, :])
           + _dot(yd_ref[...].astype(BF16), w_ref[D_SSD:D_SSD + D_DA, :])
           + _dot(yr_ref[...].astype(BF16), w_ref[D_SSD + D_DA:, :]))
    mod = mod_ref[0]
    x1 = _layer_norm(ALPHA * x_ref[...] + mod[2:3] * mix, g_ref[...], b_ref[...])
    x1_ref[...] = x1
    xm2 = x1 * (1.0 + mod[4:5]) + mod[3:4]
    xm2_ref[...] = xm2.astype(BF16)

    logits = lax.dot_general(rwt_ref[...], xm2, (((1,), (1,)), ((), ())),
                             preferred_element_type=F32, precision=HIGHEST)
    ex = jnp.exp(logits - jnp.max(logits, axis=0, keepdims=True))
    scores = ex / jnp.sum(ex, axis=0, keepdims=True)
    sel = scores + rb_ref[...]
    row = lax.broadcasted_iota(jnp.int32, sel.shape, 0)
    row_f = row.astype(F32)
    gs = []
    for g in range(N_GROUPS):
        v = [sel[EXPERTS_PER_GROUP * g + i:EXPERTS_PER_GROUP * g + i + 1, :] for i in range(EXPERTS_PER_GROUP)]
        pair_sums = [v[a] + v[b] for a in range(EXPERTS_PER_GROUP) for b in range(a + 1, EXPERTS_PER_GROUP)]
        gs.append(functools.reduce(jnp.maximum, pair_sums))
    best = functools.reduce(jnp.maximum, gs)
    grp = jnp.full(best.shape, N_GROUPS - 1, jnp.int32)
    for g in range(N_GROUPS - 2, -1, -1):
        grp = jnp.where(gs[g] == best, g, grp)
    masked = jnp.where(lax.shift_right_logical(row, 2) == grp, sel, NEG_INF)
    m1 = jnp.max(masked, axis=0, keepdims=True)
    i1 = jnp.min(jnp.where(masked == m1, row_f, float(N_EXPERTS)), axis=0, keepdims=True)
    rest = jnp.where(row_f == i1, NEG_INF, masked)
    m2 = jnp.max(rest, axis=0, keepdims=True)
    i2 = jnp.min(jnp.where(rest == m2, row_f, float(N_EXPERTS)), axis=0, keepdims=True)
    picked = jnp.where((row_f == i1) | (row_f == i2), scores, 0.0)
    combt_ref[...] = picked / jnp.sum(picked, axis=0, keepdims=True)
    grp_ref[...] = grp


def _out_proj_call(y_ssd, y_da, y_ret, x2d, mod, w_out, ln_g, ln_b, router_w, router_b, seq_len):
    n = x2d.shape[0]
    tm = 512
    per_seq = max(seq_len // tm, 1)
    mod_idx = (lambda i: (i // per_seq, 0, 0)) if mod.shape[0] > 1 else (lambda i: (0, 0, 0))
    row = lambda w: pl.BlockSpec((tm, w), lambda i: (i, 0))
    col = lambda h: pl.BlockSpec((h, tm), lambda i: (0, i))
    full = lambda s: pl.BlockSpec(s, lambda i: (0,) * len(s))
    return pl.pallas_call(
        _out_proj_kernel,
        grid=(n // tm,),
        in_specs=[row(D_SSD), row(D_DA), row(D_RET), row(D_MODEL),
                  pl.BlockSpec((1, 8, D_MODEL), mod_idx),
                  full((D_MODEL, D_MODEL)), full((1, D_MODEL)), full((1, D_MODEL)),
                  full((N_EXPERTS, D_MODEL)), full((N_EXPERTS, 1))],
        out_specs=[row(D_MODEL), row(D_MODEL), col(N_EXPERTS), col(1)],
        out_shape=[jax.ShapeDtypeStruct((n, D_MODEL), F32),
                   jax.ShapeDtypeStruct((n, D_MODEL), BF16),
                   jax.ShapeDtypeStruct((N_EXPERTS, n), F32),
                   jax.ShapeDtypeStruct((1, n), jnp.int32)],
        compiler_params=_params(("arbitrary",)),
        name="out_proj_router",
    )(y_ssd, y_da, y_ret, x2d, mod, w_out, ln_g.reshape(1, D_MODEL), ln_b.reshape(1, D_MODEL),
      router_w.T, router_b.reshape(N_EXPERTS, 1))


MOE_T = 1024
MOE_TILE_LOG2 = 6
MOE_TILE = 1 << MOE_TILE_LOG2
MOE_SLOTS = MOE_T + N_GROUPS * MOE_TILE
MOE_COL = 256
MOE_FILL_ROWS = 128
MOE_RUN_TILES = 4
MOE_EXPERTS_PER_STEP = 2


def _moe_kernel(xm_ref, combt_ref, grp_ref, wg_ref, wu_ref, wd_ref, x1_ref, mod_ref, g_ref, b_ref, o_ref,
                p_ref, xs_ref, cs_ref, y_ref, seg_ref):
    step = pl.program_id(1)
    n_tiles = MOE_SLOTS // MOE_TILE

    @pl.when(step == 0)
    def _():
        grp = grp_ref[...]
        onehot = lax.broadcasted_iota(jnp.int32, (N_GROUPS, MOE_T), 0) == grp
        onehot_f = jnp.where(onehot, 1.0, 0.0)
        bi = lax.broadcasted_iota(jnp.int32, (128, 128), 0)
        bj = lax.broadcasted_iota(jnp.int32, (128, 128), 1)
        strict_upper = jnp.where(bi < bj, 1.0, 0.0).astype(BF16)
        running = jnp.zeros((N_GROUPS, 1), F32)
        ranks = []
        for blk in range(MOE_T // 128):
            oh_b = onehot_f[:, 128 * blk:128 * (blk + 1)]
            ranks.append(_dot(oh_b.astype(BF16), strict_upper) + running)
            running = running + jnp.sum(oh_b, axis=1, keepdims=True)
        rank = jnp.concatenate(ranks, axis=1)
        tiles = lax.shift_right_logical(running.astype(jnp.int32) + (MOE_TILE - 1), MOE_TILE_LOG2)
        starts = [jnp.zeros((1, 1), jnp.int32)]
        for g in range(1, N_GROUPS):
            starts.append(starts[-1] + tiles[g - 1:g, :])
        used = starts[-1] + tiles[N_GROUPS - 1:, :]
        start = jnp.concatenate(starts, axis=0)
        pos = jnp.sum(jnp.where(onehot, (start * MOE_TILE).astype(F32) + rank, 0.0), axis=0, keepdims=True)
        pos = pos.astype(jnp.int32)
        for g in range(N_GROUPS):
            seg_ref[g] = jnp.sum(start[g:g + 1, :])
            seg_ref[N_GROUPS + g] = jnp.sum(tiles[g:g + 1, :])

        def fill(i, c):
            off = pl.multiple_of(i * MOE_FILL_ROWS, MOE_FILL_ROWS)
            slot = lax.broadcasted_iota(jnp.int32, (MOE_FILL_ROWS, MOE_T), 0) + off
            p_ref[pl.ds(off, MOE_FILL_ROWS), :] = jnp.where(slot == pos, 1.0, 0.0).astype(BF16)
            return c
        lax.fori_loop(0, MOE_SLOTS // MOE_FILL_ROWS, fill, 0)

        p = p_ref[...]
        for cb in range(D_MODEL // MOE_COL):
            cols = slice(MOE_COL * cb, MOE_COL * (cb + 1))
            xs_ref[:, cols] = _dot(p, xm_ref[:, cols]).astype(BF16)
        ct = combt_ref[...]
        hi = ct.astype(BF16)
        r1 = ct - hi.astype(F32)
        mid = r1.astype(BF16)
        lo = (r1 - mid.astype(F32)).astype(BF16)
        pieces = jnp.concatenate([hi, mid, lo], axis=0)
        perm = lax.dot_general(pieces, p, (((1,), (1,)), ((), ())), preferred_element_type=F32)
        cs_ref[...] = (perm[:N_EXPERTS] + perm[N_EXPERTS:2 * N_EXPERTS] + perm[2 * N_EXPERTS:]).T

        def clear(i, c):
            off = pl.multiple_of(i * MOE_TILE, MOE_TILE)
            y_ref[pl.ds(off, MOE_TILE), :] = jnp.zeros((MOE_TILE, D_MODEL), F32)
            return c
        lax.fori_loop(jnp.sum(used), n_tiles, clear, 0)

    lane = lax.broadcasted_iota(jnp.int32, (1, N_EXPERTS), 1)

    def run_expert(e, k):
        g_e = lax.shift_right_logical(e, 2)
        first_tile = seg_ref[g_e]
        group_tiles = seg_ref[N_GROUPS + g_e]
        first_in_group = (e & (EXPERTS_PER_GROUP - 1)) == 0

        def expert_rows(tile, rows):
            off = pl.multiple_of(tile * MOE_TILE, MOE_TILE)
            xs = xs_ref[pl.ds(off, rows), :]
            h = _silu(_dot(xs, wg_ref[0, k])) * _dot(xs, wu_ref[0, k])
            y = _dot(h.astype(BF16), wd_ref[0, k])
            w_e = jnp.sum(jnp.where(lane == e, cs_ref[pl.ds(off, rows), :], 0.0), axis=-1, keepdims=True)

            @pl.when(first_in_group)
            def _():
                y_ref[pl.ds(off, rows), :] = w_e * y

            @pl.when(jnp.logical_not(first_in_group))
            def _():
                y_ref[pl.ds(off, rows), :] += w_e * y

        def full_run(i, c):
            expert_rows(first_tile + MOE_RUN_TILES * i, MOE_RUN_TILES * MOE_TILE)
            return c
        n_full = lax.shift_right_logical(group_tiles, 2)
        lax.fori_loop(0, n_full, full_run, 0)
        rest_tile = first_tile + MOE_RUN_TILES * n_full
        for run in (2, 1):
            @pl.when((group_tiles & run) != 0)
            def _(run=run):
                done = group_tiles & (MOE_RUN_TILES - 1) & ~(2 * run - 1)
                expert_rows(rest_tile + done, run * MOE_TILE)

    for k in range(MOE_EXPERTS_PER_STEP):
        run_expert(step * MOE_EXPERTS_PER_STEP + k, k)

    @pl.when(step == N_EXPERTS // MOE_EXPERTS_PER_STEP - 1)
    def _():
        p = p_ref[...]
        for cb in range(D_MODEL // MOE_COL):
            cols = slice(MOE_COL * cb, MOE_COL * (cb + 1))
            o_ref[:, cols] = lax.dot_general(p, y_ref[:, cols].astype(BF16), (((0,), (0,)), ((), ())),
                                             preferred_element_type=F32)
        mod = mod_ref[0]
        o_ref[...] = _layer_norm(ALPHA * x1_ref[...] + mod[5:6] * o_ref[...], g_ref[...], b_ref[...])


def _moe_call(xm2, combt, grp, wg, wu, wd, layer, x1, mod, ln_g, ln_b, seq_len):
    n = xm2.shape[0]
    per_seq = max(seq_len // MOE_T, 1)
    mod_idx = (lambda i, e: (i // per_seq, 0, 0)) if mod.shape[0] > 1 else (lambda i, e: (0, 0, 0))
    row = lambda w: pl.BlockSpec((MOE_T, w), lambda i, e: (i, 0))
    col = lambda h: pl.BlockSpec((h, MOE_T), lambda i, e: (0, i))
    return pl.pallas_call(
        _moe_kernel,
        grid=(n // MOE_T, N_EXPERTS // MOE_EXPERTS_PER_STEP),
        in_specs=[row(D_MODEL), col(N_EXPERTS), col(1),
                  pl.BlockSpec((1, MOE_EXPERTS_PER_STEP, D_MODEL, D_FF), lambda i, e: (layer, e, 0, 0)),
                  pl.BlockSpec((1, MOE_EXPERTS_PER_STEP, D_MODEL, D_FF), lambda i, e: (layer, e, 0, 0)),
                  pl.BlockSpec((1, MOE_EXPERTS_PER_STEP, D_FF, D_MODEL), lambda i, e: (layer, e, 0, 0)),
                  row(D_MODEL),
                  pl.BlockSpec((1, 8, D_MODEL), mod_idx),
                  pl.BlockSpec((1, D_MODEL), lambda i, e: (0, 0)),
                  pl.BlockSpec((1, D_MODEL), lambda i, e: (0, 0))],
        out_specs=row(D_MODEL),
        out_shape=jax.ShapeDtypeStruct((n, D_MODEL), F32),
        scratch_shapes=[pltpu.VMEM((MOE_SLOTS, MOE_T), BF16),
                        pltpu.VMEM((MOE_SLOTS, D_MODEL), BF16),
                        pltpu.VMEM((MOE_SLOTS, N_EXPERTS), F32),
                        pltpu.VMEM((MOE_SLOTS, D_MODEL), F32),
                        pltpu.SMEM((2 * N_GROUPS,), jnp.int32)],
        compiler_params=_params(("arbitrary", "arbitrary")),
        name="moe_grouped",
    )(xm2, combt, grp, wg, wu, wd, x1, mod, ln_g.reshape(1, D_MODEL), ln_b.reshape(1, D_MODEL))


def _pack_in_proj(w_in_l, rope):
    z, xbc, dtw, da, ret = jnp.split(w_in_l, (512, 1280, 1296, 2064), axis=1)
    wq, wk, wv = jnp.split(da, 3, axis=1)
    wrq, wrk, wrv, wrg = jnp.split(ret, 4, axis=1)
    nn_w = {"z": z, "xbc": xbc, "k": wk, "rq": wrq, "rv": wrv, "rg": wrg, "v": wv,
            "dt": jnp.pad(dtw, ((0, 0), (0, DT_COLS - dtw.shape[1])))}
    nt_w = {"q": wq.T, "v": wv.T, "rk": wrk.T, "dt": jnp.pad(dtw.T, ((0, DT_ROWS - dtw.shape[1]), (0, 0)))}
    cols, rows = _in_proj_layout(rope)
    return (jnp.concatenate([nn_w[name] for name in cols], axis=1).astype(BF16),
            jnp.concatenate([nt_w[name] for name in rows], axis=0).astype(BF16))


def _rope_tables(seq_len, dim, reps):
    nf = dim // 4
    inv = ROPE_BASE ** (-jnp.arange(nf, dtype=F32) / nf)
    t = jnp.arange(seq_len)
    r = (t // GRID_W).astype(F32)[:, None] * inv
    c = (t % GRID_W).astype(F32)[:, None] * inv
    ang = jnp.tile(jnp.concatenate([r, r, c, c], axis=-1), (1, reps))
    return jnp.cos(ang), jnp.sin(ang)


def _layer(x, mod, l, P, ctx, rope_tabs):
    bsz, seq_len, _ = x.shape
    n = bsz * seq_len
    x2d = x.reshape(n, D_MODEL)
    latent = ctx is not None
    wnn, wnt = P["in_proj"][l][1 if latent else 0]
    outs = _in_proj_call(x2d, mod, wnn, wnt, seq_len, rope_tabs)
    z, xbc, dt, dtt, qt, k, vt, rq, rkt, rv, rg = outs[:11]
    r3 = lambda a: a.reshape(bsz, seq_len, a.shape[-1])

    h_ssd0 = ctx[2] if latent else None
    h_ret0 = ctx[3] if latent else None
    ssd_out = _ssd_call(r3(xbc), r3(z), r3(dt), dtt, P["ssd_conv_w"][l], P["ssd_conv_b"][l],
                        P["ssd_dt_bias"][l], P["ssd_a_log"][l], P["ssd_d"][l], P["ssd_norm_w"][l], h_ssd0)
    ret_out = _ret_call(r3(rq), rkt, r3(rv), r3(rg), P["ret_decay"][l], P["ret_norm_w"][l], h_ret0)

    ks, vts = [r3(k)], [vt]
    if latent:
        y_ssd, y_ret = ssd_out, ret_out
        new = None
        ks.append(ctx[0].reshape(bsz, -1, D_DA).astype(BF16))
        vts.append(jnp.transpose(ctx[1].reshape(bsz, -1, D_DA), (0, 2, 1)).astype(BF16))
    else:
        y_ssd, hs = ssd_out
        y_ret, hr = ret_out
        v = outs[11]
        new = (k.reshape(bsz, seq_len, H_DA, DA_HEAD_DIM), v.reshape(bsz, seq_len, H_DA, DA_HEAD_DIM), hs, hr)
    lam_init = 0.8 - 0.6 * math.exp(-0.3 * l)
    y_da = _da_call(qt, ks, vts, P["da_lambda"][l], P["da_norm_w"][l], lam_init)

    x1, xm2, combt, grp = _out_proj_call(y_ssd.reshape(n, D_SSD), y_da.reshape(n, D_DA),
                                         y_ret.reshape(n, D_RET), x2d, mod, P["w_out16"][l],
                                         P["ln_mix_g"][l], P["ln_mix_b"][l], P["router_w"], P["router_b"],
                                         seq_len)
    x2 = _moe_call(xm2, combt, grp, P["wg16"], P["wu16"], P["wd16"], l, x1, mod,
                   P["ln_ffn_g"][l], P["ln_ffn_b"][l], seq_len)
    return x2.reshape(bsz, seq_len, D_MODEL), new


def kernel(x_prompt, x_sample, c, cache_da_k, cache_da_v, state_ssd, state_ret, c_ctx, w_ada, b_ada, w_in,
           ssd_conv_w, ssd_conv_b, ssd_dt_bias, ssd_a_log, ssd_d, ssd_norm_w, da_lambda, da_norm_w, ret_decay,
           ret_norm_w, w_out, ln_mix_g, ln_mix_b, router_w, router_b, moe_w_gate, moe_w_up, moe_w_down,
           ln_ffn_g, ln_ffn_b):
    dec_b = x_sample.shape[0]
    P = dict(ssd_conv_w=ssd_conv_w, ssd_conv_b=ssd_conv_b, ssd_dt_bias=ssd_dt_bias, ssd_a_log=ssd_a_log,
             ssd_d=ssd_d, ssd_norm_w=ssd_norm_w, da_lambda=da_lambda, da_norm_w=da_norm_w,
             ret_decay=ret_decay, ret_norm_w=ret_norm_w, ln_mix_g=ln_mix_g, ln_mix_b=ln_mix_b,
             router_w=router_w, router_b=router_b, ln_ffn_g=ln_ffn_g, ln_ffn_b=ln_ffn_b)
    P["in_proj"] = [(_pack_in_proj(w_in[l], False), _pack_in_proj(w_in[l], True)) for l in range(DEPTH)]
    P["w_out16"] = w_out.astype(BF16)
    P["wg16"] = moe_w_gate.astype(BF16)
    P["wu16"] = moe_w_up.astype(BF16)
    P["wd16"] = moe_w_down.astype(BF16)

    cond8 = jnp.concatenate([c_ctx[None, :], c, jnp.zeros((8 - 1 - dec_b, D_MODEL), F32)], axis=0)
    mod_all = _ada_call(cond8, w_ada, b_ada).reshape(DEPTH, 8, 6, D_MODEL)
    mod_all = jnp.pad(mod_all, ((0, 0), (0, 0), (0, 2), (0, 0)))

    y = x_prompt
    ks_, vs_, hs_, hr_ = [], [], [], []
    for l in range(DEPTH):
        y, (k_l, v_l, hs_l, hr_l) = _layer(y, mod_all[l, 0:1], l, P, None, None)
        ks_.append(k_l)
        vs_.append(v_l)
        hs_.append(hs_l)
        hr_.append(hr_l)
    def stack_layers(parts):
        bsz = parts[0].shape[0]
        flat = jnp.concatenate([p.reshape(bsz, 1, -1) for p in parts], axis=1)
        return flat.reshape((bsz, len(parts)) + parts[0].shape[1:])

    new_da_k = stack_layers(ks_)
    new_da_v = stack_layers(vs_)
    new_ssd = stack_layers(hs_)
    new_ret = stack_layers(hr_)

    seq_len = x_sample.shape[1]
    cos_da, sin_da = _rope_tables(seq_len, DA_HALF, D_DA // DA_HALF)
    cos_ret, sin_ret = _rope_tables(seq_len, RET_HEAD_DIM, H_RET)
    rope_tabs = (cos_da, sin_da, cos_ret, sin_ret, cos_da.T, sin_da.T, cos_ret.T, sin_ret.T)
    zl = x_sample
    for l in range(DEPTH):
        ctx = (cache_da_k[:, l], cache_da_v[:, l], state_ssd[:, l], state_ret[:, l])
        zl, _ = _layer(zl, mod_all[l, 1:1 + dec_b], l, P, ctx, rope_tabs)

    return (y, zl, new_da_k, new_da_v, new_ssd, new_ret)
```

```python
import functools
import math

import jax
import jax.numpy as jnp
import numpy as np
from jax import lax
from jax.experimental import pallas as pl
from jax.experimental.pallas import tpu as pltpu

D_MODEL = 1024
DEPTH = 2
GRID_W = 64
CHUNK = 128
H_SSD = 8
SSD_HEAD_DIM = 64
D_SSD = H_SSD * SSD_HEAD_DIM
SSD_GROUPS = 2
D_STATE = 64
CONV_W = 5
D_XBC = D_SSD + 2 * SSD_GROUPS * D_STATE
H_DA = 4
DA_HALF = 32
DA_HEAD_DIM = 2 * DA_HALF
D_DA = H_DA * DA_HEAD_DIM
H_RET = 4
RET_HEAD_DIM = 64
D_RET = H_RET * RET_HEAD_DIM
N_EXPERTS = 16
EXPERTS_PER_GROUP = 4
N_GROUPS = N_EXPERTS // EXPERTS_PER_GROUP
D_FF = 512
ROPE_BASE = 10000.0
EPS = 1e-6
ALPHA = (2 * DEPTH) ** 0.25

F32 = jnp.float32
BF16 = jnp.bfloat16
HIGHEST = lax.Precision.HIGHEST
NEG_INF = float("-inf")
LOG2_E = 1.4426950408889634

V7X_VMEM_LIMIT_BYTES = 56 * 1024 * 1024
CONV_HALO = 8

DT_COLS = 128
DT_ROWS = 32


def _in_proj_layout(rope):
    nn = [("z", D_SSD), ("xbc", D_XBC), ("k", D_DA), ("rq", D_RET), ("rv", D_RET), ("rg", D_RET), ("dt", DT_COLS)]
    nt = [("q", D_DA), ("v", D_DA), ("rk", D_RET), ("dt", DT_ROWS)]
    if not rope:
        nn += [("v", D_DA)]

    def spans(parts):
        out, at = {}, 0
        for name, width in parts:
            out[name] = (at, at + width)
            at += width
        return out
    return spans(nn), spans(nt)


def _silu(x):
    return x * (1.0 / (1.0 + jnp.exp(-x)))


def _softplus(x):
    return jnp.maximum(x, 0.0) + jnp.log1p(jnp.exp(-jnp.abs(x)))


def _dot(a, b, precision=None):
    return jnp.dot(a, b, preferred_element_type=F32, precision=precision)


def _layer_norm(t, g, b):
    mu = jnp.mean(t, axis=-1, keepdims=True)
    tc = t - mu
    var = jnp.mean(tc * tc, axis=-1, keepdims=True)
    return tc * lax.rsqrt(var + EPS) * g + b


def _params(sem):
    return pltpu.CompilerParams(dimension_semantics=sem, vmem_limit_bytes=V7X_VMEM_LIMIT_BYTES)


def _ada_kernel(cond_ref, w_ref, b_ref, o_ref):
    o_ref[0] = _dot(_silu(cond_ref[...]), w_ref[0], HIGHEST) + b_ref[0]


def _ada_call(cond8, w_ada, b_ada):
    tn = 1536
    nb = (6 * D_MODEL) // tn
    return pl.pallas_call(
        _ada_kernel,
        grid=(DEPTH, nb),
        in_specs=[pl.BlockSpec((8, D_MODEL), lambda l, n: (0, 0)),
                  pl.BlockSpec((1, D_MODEL, tn), lambda l, n: (l, 0, n)),
                  pl.BlockSpec((1, 1, tn), lambda l, n: (l, 0, n))],
        out_specs=pl.BlockSpec((1, 8, tn), lambda l, n: (l, 0, n)),
        out_shape=jax.ShapeDtypeStruct((DEPTH, 8, 6 * D_MODEL), F32),
        compiler_params=_params(("arbitrary", "arbitrary")),
        name="ada_mod",
    )(cond8, w_ada, b_ada.reshape(DEPTH, 1, 6 * D_MODEL))


def _rot_rows(x, dim):
    nf, m = dim // 4, dim // 2
    parts = []
    for b in range(0, x.shape[0], dim):
        parts += [-x[b + nf:b + m], x[b:b + nf], -x[b + m + nf:b + dim], x[b + m:b + m + nf]]
    return jnp.concatenate(parts, axis=0)


def _rot_lanes(x, dim):
    nf, n = dim // 4, x.shape[1]
    lane = lax.broadcasted_iota(jnp.int32, (1, n), 1)
    ahead = pltpu.roll(x, n - nf, axis=1)
    behind = pltpu.roll(x, nf, axis=1)
    return jnp.where((lane & (2 * nf - 1)) < nf, -ahead, behind)


def _in_proj_kernel(rope, n_alias, *refs):
    if rope:
        (x_ref, mod_ref, wnn_ref, wnt_ref, cd_ref, sd_ref, cr_ref, sr_ref,
         cdt_ref, sdt_ref, crt_ref, srt_ref,
         z_ref, xbc_ref, dt_ref, dtt_ref, qt_ref, k_ref, vt_ref, rq_ref, rkt_ref, rv_ref, rg_ref) = refs
    else:
        refs = refs[:4] + refs[4 + n_alias:]
        (x_ref, mod_ref, wnn_ref, wnt_ref,
         z_ref, xbc_ref, dt_ref, dtt_ref, qt_ref, k_ref, vt_ref, rq_ref, rkt_ref, rv_ref, rg_ref,
         v_ref) = refs
    cols, rows = _in_proj_layout(rope)
    mod = mod_ref[0]
    xm = (x_ref[...] * (1.0 + mod[1:2]) + mod[0:1]).astype(BF16)

    def nn(name):
        a, b = cols[name]
        return _dot(xm, wnn_ref[:, a:b])

    def nt(name):
        a, b = rows[name]
        return lax.dot_general(wnt_ref[a:b, :], xm, (((1,), (1,)), ((), ())),
                               preferred_element_type=F32)

    z_ref[...] = nn("z")
    xbc_ref[...] = nn("xbc")
    dt_ref[...] = nn("dt")[:, :2 * H_SSD]
    dtt_ref[0] = nt("dt")[:2 * H_SSD, :]
    vt_ref[0] = nt("v").astype(BF16)
    rv_ref[...] = nn("rv")
    rg_ref[...] = nn("rg")
    qt = nt("q")
    k = nn("k")
    rq = nn("rq")
    rkt = nt("rk")
    if rope:
        qt = qt * cdt_ref[...] + _rot_rows(qt, DA_HALF) * sdt_ref[...]
        k = k * cd_ref[...] + _rot_lanes(k, DA_HALF) * sd_ref[...]
        rq = rq * cr_ref[...] + _rot_lanes(rq, RET_HEAD_DIM) * sr_ref[...]
        rkt = rkt * crt_ref[...] + _rot_rows(rkt, RET_HEAD_DIM) * srt_ref[...]
    else:
        v_ref[...] = nn("v")
    qt_ref[0] = (qt * (DA_HALF ** -0.5 * LOG2_E)).astype(BF16)
    k_ref[...] = k.astype(k_ref.dtype)
    rq_ref[...] = (rq * (RET_HEAD_DIM ** -0.5)).astype(BF16)
    rkt_ref[0] = rkt.astype(BF16)


def _in_proj_call(x2d, mod, wnn, wnt, seq_len, rope_tabs, layer=0, kv_stacks=None):
    n = x2d.shape[0]
    tm = min(512, seq_len)
    nblk = n // tm
    per_seq = seq_len // tm
    n_mod = mod.shape[0]
    rope = rope_tabs is not None
    mod_idx = (lambda i: (i // per_seq, 0, 0)) if n_mod > 1 else (lambda i: (0, 0, 0))
    bsz = n // seq_len
    row = lambda w: pl.BlockSpec((tm, w), lambda i: (i, 0))
    col = lambda h: pl.BlockSpec((1, h, tm), lambda i: (i // per_seq, 0, i % per_seq))
    in_specs = [row(D_MODEL),
                pl.BlockSpec((1, 8, D_MODEL), mod_idx),
                pl.BlockSpec(wnn.shape, lambda i: (0, 0)),
                pl.BlockSpec(wnt.shape, lambda i: (0, 0))]
    args = [x2d, mod, wnn, wnt]
    if rope:
        in_specs += [pl.BlockSpec((tm, 256), lambda i: (i % per_seq, 0))] * 4
        in_specs += [pl.BlockSpec((256, tm), lambda i: (0, i % per_seq))] * 4
        args += list(rope_tabs)
    sds = jax.ShapeDtypeStruct
    out_shape = [sds((n, D_SSD), F32), sds((n, D_XBC), F32), sds((n, 2 * H_SSD), F32),
                 sds((bsz, 2 * H_SSD, seq_len), F32), sds((bsz, D_DA, seq_len), BF16),
                 sds((n, D_DA), BF16 if rope else F32), sds((bsz, D_DA, seq_len), BF16),
                 sds((n, D_RET), BF16), sds((bsz, D_RET, seq_len), BF16),
                 sds((n, D_RET), F32), sds((n, D_RET), F32)]
    out_specs = [row(D_SSD), row(D_XBC), row(2 * H_SSD), col(2 * H_SSD), col(D_DA), row(D_DA),
                 col(D_DA), row(D_RET), col(D_RET), row(D_RET), row(D_RET)]
    aliases = {}
    if not rope:
        assert tm == seq_len
        slot = pl.BlockSpec((tm, D_DA), lambda i: (DEPTH * i + layer, 0))
        k_out, v_out = 5, len(out_shape)
        out_shape[k_out] = sds((DEPTH * n, D_DA), F32)
        out_specs[k_out] = slot
        out_shape.append(sds((DEPTH * n, D_DA), F32))
        out_specs.append(slot)
        if kv_stacks is not None:
            aliases = {len(args): k_out, len(args) + 1: v_out}
            in_specs += [pl.BlockSpec(memory_space=pl.ANY)] * 2
            args += list(kv_stacks)
    return pl.pallas_call(
        functools.partial(_in_proj_kernel, rope, len(aliases)),
        grid=(nblk,),
        in_specs=in_specs,
        out_specs=out_specs,
        out_shape=out_shape,
        input_output_aliases=aliases,
        compiler_params=_params(("arbitrary",)),
        name="in_proj_rope" if rope else "in_proj",
    )(*args)


SEQS_PER_STEP = 4


def _seqs_per_step(bsz):
    assert bsz % SEQS_PER_STEP == 0
    return SEQS_PER_STEP


SSD_SEQ_REFS = (0, 1, 2, 3, 4, 5)
RET_SEQ_REFS = (0, 1, 2, 3)


def _per_sequence(seq_kernel, n_in, seq_in, n_scratch, nc, nb, has_h0, *refs):
    ins = list(refs[:n_in])
    outs = list(refs[n_in:len(refs) - n_scratch])
    scratch = list(refs[len(refs) - n_scratch:])
    live = []
    for s in range(nb):
        one = lambda r, s=s: r.at[pl.ds(s, 1)]
        seq_ins = [one(r) if i in seq_in else r for i, r in enumerate(ins)]
        live.append(seq_kernel(nc, has_h0, *seq_ins, *[one(r) for r in outs], *[r.at[s] for r in scratch]))
    while live:
        still = []
        for gen in live:
            try:
                next(gen)
                still.append(gen)
            except StopIteration:
                pass
        live = still


CONV_SHIFT_TAPS = tuple(k for k in range(CONV_W) if k != CONV_W // 2)
CONV_EXT_ROWS = CHUNK + 2 * CONV_HALO
CONV_SHIFT_SHAPE = (len(CONV_SHIFT_TAPS) * CHUNK, 2 * CONV_EXT_ROWS)


def _conv_shift_matrix():
    s = np.zeros(CONV_SHIFT_SHAPE, np.float32)
    t = np.arange(CHUNK)
    for idx, k in enumerate(CONV_SHIFT_TAPS):
        src = CONV_HALO + t + k - CONV_W // 2
        s[idx * CHUNK + t, src] = 1.0
        s[idx * CHUNK + t, CONV_EXT_ROWS + src] = 1.0
    return jnp.asarray(s, dtype=BF16)


def _ssd_seq(nc, has_h0, *refs):
    refs = list(refs)
    (xm_ref, xp_ref, xn_ref, z_ref, dt_ref, dtt_ref, cw_ref, cb_ref, dtb_c_ref, dtb_r_ref,
     alog_c_ref, alog_r_ref, dskip_ref, nw_ref, shift_ref) = refs[:15]
    refs = refs[15:]
    h0_ref = refs.pop(0) if has_h0 else None
    y_ref = refs.pop(0)
    hs_ref = None if has_h0 else refs.pop(0)
    hp_ref, yf_ref = refs

    d = pl.program_id(1)
    j = pl.program_id(2)
    fwd = d == 0
    cj = jnp.where(fwd, j, nc - 1 - j)

    @pl.when(j == 0)
    def _():
        for p in range(H_SSD // 2):
            if has_h0:
                g = p // 2
                both = jnp.concatenate([h0_ref[0, pl.ds(d, 1), 2 * p][0],
                                        h0_ref[0, pl.ds(d, 1), 2 * p + 1][0]], axis=1)
                zero = jnp.zeros((D_STATE, 128), F32)
                hp_ref[p] = jnp.concatenate([both, zero] if g == 0 else [zero, both], axis=0)
            else:
                hp_ref[p] = jnp.zeros((128, 128), F32)
    yield

    x_c = xm_ref[0]
    ext = jnp.concatenate([jnp.where(cj > 0, xp_ref[0], 0.0), x_c, jnp.where(cj < nc - 1, xn_ref[0], 0.0)],
                          axis=0)
    hi = ext.astype(BF16)
    lo = (ext - hi.astype(F32)).astype(BF16)
    shifted = _dot(shift_ref[...], jnp.concatenate([hi, lo], axis=0))
    acc = cb_ref[...] + cw_ref[CONV_W // 2:CONV_W // 2 + 1, :] * x_c
    for idx, k in enumerate(CONV_SHIFT_TAPS):
        acc = acc + cw_ref[k:k + 1, :] * shifted[CHUNK * idx:CHUNK * (idx + 1), :]
    act = _silu(acc)
    xs = act[:, :D_SSD]
    bm = act[:, D_SSD:D_SSD + 128]
    cm = act[:, D_SSD + 128:]
    yield

    dt_c_all = _softplus(dt_ref[0] + dtb_c_ref[...])
    dt_r_all = _softplus(dtt_ref[0] + dtb_r_ref[...])
    a_c_all = -jnp.exp(alog_c_ref[...])
    a_r_all = -jnp.exp(alog_r_ref[...])
    dt_c = jnp.where(fwd, dt_c_all[:, :H_SSD], dt_c_all[:, H_SSD:])
    dt_r = jnp.where(fwd, dt_r_all[:H_SSD], dt_r_all[H_SSD:])
    la_c = dt_c * jnp.where(fwd, a_c_all[:, :H_SSD], a_c_all[:, H_SSD:])
    la_r = dt_r * jnp.where(fwd, a_r_all[:H_SSD], a_r_all[H_SSD:])

    ii = lax.broadcasted_iota(jnp.int32, (CHUNK, CHUNK), 0)
    jj = lax.broadcasted_iota(jnp.int32, (CHUNK, CHUNK), 1)
    valid = jnp.where(fwd, ii - jj, jj - ii) >= 0
    valid_t = jnp.where(fwd, jj - ii, ii - jj) >= 0
    cum_c = _dot(valid.astype(F32), la_c, HIGHEST)
    cum_r = _dot(la_r, valid_t.astype(F32), HIGHEST)
    tot_r = jnp.where(fwd, cum_r[:, CHUNK - 1:], cum_r[:, :1])
    e_tot = jnp.exp(tot_r)
    wk_r = dt_r * jnp.exp(tot_r - cum_r)
    e_cum_c = jnp.exp(cum_c)
    cum2_c = cum_c * LOG2_E
    cum2_r = (cum_r - jnp.log(dt_r)) * LOG2_E

    yield
    bm_t = bm.T
    bm_t16 = bm_t.astype(BF16)
    lane = lax.broadcasted_iota(jnp.int32, (1, 128), 1)
    low_lanes = lane < 64
    feat_row = lax.broadcasted_iota(jnp.int32, (128, 1), 0)

    ys = []
    for g in range(SSD_GROUPS):
        in_group_lane = (lane >= 64 * g) & (lane < 64 * (g + 1))
        cg = jnp.where(in_group_lane, cm, 0.0)
        gram = _dot(cg.astype(BF16), bm_t16)
        in_group_row = (feat_row >= 64 * g) & (feat_row < 64 * (g + 1))
        for p in range(2 * g, 2 * g + 2):
            xs_p = xs[:, 128 * p:128 * (p + 1)].astype(BF16)
            h_prev = hp_ref[p]
            h_prev16 = h_prev.astype(BF16)
            y_h, s_h = [], []
            for hh in (2 * p, 2 * p + 1):
                decay_dt = jnp.exp2(jnp.where(valid, cum2_c[:, hh:hh + 1] - cum2_r[hh:hh + 1, :], NEG_INF))
                sc = gram * decay_dt
                y_intra = _dot(sc.astype(BF16), xs_p)
                y_inter = _dot((cg * e_cum_c[:, hh:hh + 1]).astype(BF16), h_prev16)
                y_h.append(y_intra + y_inter)
                s_h.append(_dot((bm_t * wk_r[hh:hh + 1, :]).astype(BF16), xs_p))
            ys.append(jnp.where(low_lanes, y_h[0], y_h[1]))
            s_pair = jnp.where(low_lanes, s_h[0], s_h[1])
            e_pair = jnp.where(low_lanes, e_tot[2 * p:2 * p + 1, :], e_tot[2 * p + 1:2 * p + 2, :])
            hp_ref[p] = jnp.where(in_group_row, e_pair * h_prev + s_pair, 0.0)
            yield
    y_dir = jnp.concatenate(ys, axis=1)

    if hs_ref is not None:
        @pl.when(j == nc - 1)
        def _():
            for dd in range(2):
                @pl.when(d == dd)
                def _():
                    for p in range(H_SSD // 2):
                        g = p // 2
                        hn = hp_ref[p]
                        hs_ref[0, dd, 2 * p] = hn[64 * g:64 * (g + 1), :64]
                        hs_ref[0, dd, 2 * p + 1] = hn[64 * g:64 * (g + 1), 64:]

    @pl.when(fwd)
    def _():
        yf_ref[cj] = y_dir.astype(yf_ref.dtype)

    @pl.when(d == 1)
    def _():
        yt = yf_ref[cj].astype(F32) + y_dir + dskip_ref[...] * xs
        gated = yt * _silu(z_ref[0])
        ms = jnp.mean(gated * gated, axis=-1, keepdims=True)
        y_ref[0] = gated * lax.rsqrt(ms + EPS) * nw_ref[...]


def _ssd_call(xbc, z, dt, dtt, conv_w, conv_b, dt_bias, a_log, d_skip, norm_w, h0):
    bsz, seq_len, _ = xbc.shape
    nc = seq_len // CHUNK
    has_h0 = h0 is not None
    hb = CHUNK // CONV_HALO
    last_hb = seq_len // CONV_HALO - 1

    def chunk_of(d, j):
        return jnp.where(d == 0, j, nc - 1 - j)

    def full(shape):
        return pl.BlockSpec(shape, lambda b, d, j: (0,) * len(shape))

    nb = _seqs_per_step(bsz)
    in_specs = [
        pl.BlockSpec((nb, CHUNK, D_XBC), lambda b, d, j: (b, chunk_of(d, j), 0)),
        pl.BlockSpec((nb, CONV_HALO, D_XBC), lambda b, d, j: (b, jnp.maximum(chunk_of(d, j) * hb - 1, 0), 0)),
        pl.BlockSpec((nb, CONV_HALO, D_XBC),
                     lambda b, d, j: (b, jnp.minimum((chunk_of(d, j) + 1) * hb, last_hb), 0)),
        pl.BlockSpec((nb, CHUNK, D_SSD), lambda b, d, j: (b, chunk_of(d, j), 0)),
        pl.BlockSpec((nb, CHUNK, 2 * H_SSD), lambda b, d, j: (b, chunk_of(d, j), 0)),
        pl.BlockSpec((nb, 2 * H_SSD, CHUNK), lambda b, d, j: (b, 0, chunk_of(d, j))),
        full((CONV_W, D_XBC)), full((1, D_XBC)), full((1, 2 * H_SSD)), full((2 * H_SSD, 1)),
        full((1, 2 * H_SSD)), full((2 * H_SSD, 1)), full((1, D_SSD)), full((1, D_SSD)),
        full(CONV_SHIFT_SHAPE),
    ]
    args = [xbc, xbc, xbc, z, dt, dtt, conv_w, conv_b.reshape(1, D_XBC),
            dt_bias.reshape(1, 2 * H_SSD), dt_bias.reshape(2 * H_SSD, 1),
            a_log.reshape(1, 2 * H_SSD), a_log.reshape(2 * H_SSD, 1),
            jnp.repeat(d_skip, SSD_HEAD_DIM).reshape(1, D_SSD), norm_w.reshape(1, D_SSD),
            _conv_shift_matrix()]
    y_spec = pl.BlockSpec((nb, CHUNK, D_SSD), lambda b, d, j: (b, jnp.where(d == 0, nc - 1, nc - 1 - j), 0))
    y_shape = jax.ShapeDtypeStruct((bsz, seq_len, D_SSD), F32)
    state_block = (nb, 2, H_SSD, D_STATE, SSD_HEAD_DIM)
    state_spec = pl.BlockSpec(state_block, lambda b, d, j: (b, 0, 0, 0, 0))
    seq_in = SSD_SEQ_REFS
    if has_h0:
        seq_in = seq_in + (len(in_specs),)
        in_specs.append(state_spec)
        args.append(h0)
        out_specs, out_shape = y_spec, y_shape
    else:
        out_specs = [y_spec, state_spec]
        out_shape = [y_shape, jax.ShapeDtypeStruct((bsz,) + state_block[1:], F32)]
    return pl.pallas_call(
        functools.partial(_per_sequence, _ssd_seq, len(in_specs), seq_in, 2, nc, nb, has_h0),
        grid=(bsz // nb, 2, nc),
        in_specs=in_specs,
        out_specs=out_specs,
        out_shape=out_shape,
        scratch_shapes=[pltpu.VMEM((nb, H_SSD // 2, 128, 128), F32),
                        pltpu.VMEM((nb, nc, CHUNK, D_SSD), BF16)],
        compiler_params=_params(("arbitrary", "arbitrary", "arbitrary")),
        name="ssd_latent" if has_h0 else "ssd_ctx",
    )(*args)


def _ret_seq(nc, has_h0, *refs):
    refs = list(refs)
    q_ref, kt_ref, v_ref, g_ref, dec_ref, nw_ref = refs[:6]
    refs = refs[6:]
    h0_ref = refs.pop(0) if has_h0 else None
    y_ref = refs.pop(0)
    hs_ref = None if has_h0 else refs.pop(0)
    hr_ref, yf_ref = refs

    d = pl.program_id(1)
    j = pl.program_id(2)
    fwd = d == 0
    cj = jnp.where(fwd, j, nc - 1 - j)

    @pl.when(j == 0)
    def _():
        for p in range(H_RET // 2):
            if has_h0:
                zero = jnp.zeros((RET_HEAD_DIM, RET_HEAD_DIM), F32)
                top = jnp.concatenate([h0_ref[0, pl.ds(d, 1), 2 * p][0], zero], axis=1)
                bot = jnp.concatenate([zero, h0_ref[0, pl.ds(d, 1), 2 * p + 1][0]], axis=1)
                hr_ref[p] = jnp.concatenate([top, bot], axis=0)
            else:
                hr_ref[p] = jnp.zeros((128, 128), F32)
    yield

    dec = dec_ref[...]
    lg_all = -_softplus(-dec)
    lg = jnp.where(fwd, lg_all[0:1], lg_all[1:2])

    ii = lax.broadcasted_iota(jnp.int32, (CHUNK, CHUNK), 0)
    jj = lax.broadcasted_iota(jnp.int32, (CHUNK, CHUNK), 1)
    dist = jnp.where(fwd, ii - jj, jj - ii)
    valid = dist >= 0
    dist_f = dist.astype(F32)
    qpos = lax.broadcasted_iota(jnp.int32, (CHUNK, 1), 0)
    kpos = lax.broadcasted_iota(jnp.int32, (1, CHUNK), 1)
    n_q = jnp.where(fwd, qpos + 1, CHUNK - qpos).astype(F32)
    n_k = jnp.where(fwd, CHUNK - 1 - kpos, kpos).astype(F32)
    lane = lax.broadcasted_iota(jnp.int32, (1, 128), 1)
    low_lanes = lane < 64
    low_rows = lax.broadcasted_iota(jnp.int32, (128, 1), 0) < 64
    block_diag = low_rows == low_lanes

    ys = []
    for p in range(H_RET // 2):
        q_p = q_ref[0, :, 128 * p:128 * (p + 1)].astype(F32)
        kt_p = kt_ref[0, 128 * p:128 * (p + 1), :]
        v_p = v_ref[0, :, 128 * p:128 * (p + 1)].astype(BF16)
        h_prev = hr_ref[p]
        lg_a = lg[:, 2 * p:2 * p + 1]
        lg_b = lg[:, 2 * p + 1:2 * p + 2]
        y_h = []
        for lg_h, in_head in ((lg_a, low_lanes), (lg_b, jnp.logical_not(low_lanes))):
            s = _dot(jnp.where(in_head, q_p, 0.0).astype(BF16), kt_p)
            decay = jnp.exp(jnp.where(valid, dist_f * lg_h, NEG_INF))
            y_h.append(_dot((s * decay).astype(BF16), v_p))
        e_q = jnp.where(low_lanes, jnp.exp(n_q * lg_a), jnp.exp(n_q * lg_b))
        y_inter = _dot((q_p * e_q).astype(BF16), h_prev.astype(BF16))
        w_k = jnp.where(low_rows, jnp.exp(n_k * lg_a), jnp.exp(n_k * lg_b))
        s_new = _dot((kt_p.astype(F32) * w_k).astype(BF16), v_p)
        e_tot = jnp.where(low_lanes, jnp.exp(CHUNK * lg_a), jnp.exp(CHUNK * lg_b))
        hr_ref[p] = e_tot * h_prev + jnp.where(block_diag, s_new, 0.0)
        ys.append(jnp.where(low_lanes, y_h[0], y_h[1]) + y_inter)
        yield
    y_dir = jnp.concatenate(ys, axis=1)

    if hs_ref is not None:
        @pl.when(j == nc - 1)
        def _():
            for dd in range(2):
                @pl.when(d == dd)
                def _():
                    for p in range(H_RET // 2):
                        hn = hr_ref[p]
                        hs_ref[0, dd, 2 * p] = hn[:64, :64]
                        hs_ref[0, dd, 2 * p + 1] = hn[64:, 64:]

    @pl.when(fwd)
    def _():
        yf_ref[cj] = y_dir.astype(yf_ref.dtype)

    @pl.when(d == 1)
    def _():
        yt = yf_ref[cj].astype(F32) + y_dir
        lane4 = lax.shift_right_logical(lax.broadcasted_iota(jnp.int32, (1, D_RET), 1), 6)
        mu = jnp.zeros_like(yt)
        for h in range(H_RET):
            m_h = jnp.sum(jnp.where(lane4 == h, yt, 0.0), axis=-1, keepdims=True) * (1.0 / RET_HEAD_DIM)
            mu = jnp.where(lane4 == h, m_h, mu)
        yc = yt - mu
        sq = yc * yc
        rs = jnp.zeros_like(yt)
        for h in range(H_RET):
            v_h = jnp.sum(jnp.where(lane4 == h, sq, 0.0), axis=-1, keepdims=True) * (1.0 / RET_HEAD_DIM)
            rs = jnp.where(lane4 == h, lax.rsqrt(v_h + EPS), rs)
        y_ref[0] = _silu(g_ref[0]) * (yc * rs * nw_ref[...])


def _ret_call(rq, rkt, rv, rg, ret_decay, norm_w, h0):
    bsz, seq_len, _ = rq.shape
    nc = seq_len // CHUNK
    has_h0 = h0 is not None

    def chunk_of(d, j):
        return jnp.where(d == 0, j, nc - 1 - j)

    nb = _seqs_per_step(bsz)
    tok = pl.BlockSpec((nb, CHUNK, D_RET), lambda b, d, j: (b, chunk_of(d, j), 0))
    in_specs = [tok,
                pl.BlockSpec((nb, D_RET, CHUNK), lambda b, d, j: (b, 0, chunk_of(d, j))),
                tok, tok,
                pl.BlockSpec((2, H_RET), lambda b, d, j: (0, 0)),
                pl.BlockSpec((1, D_RET), lambda b, d, j: (0, 0))]
    args = [rq, rkt, rv, rg, ret_decay, jnp.tile(norm_w, H_RET).reshape(1, D_RET)]
    y_spec = pl.BlockSpec((nb, CHUNK, D_RET), lambda b, d, j: (b, jnp.where(d == 0, nc - 1, nc - 1 - j), 0))
    y_shape = jax.ShapeDtypeStruct((bsz, seq_len, D_RET), F32)
    state_block = (nb, 2, H_RET, RET_HEAD_DIM, RET_HEAD_DIM)
    state_spec = pl.BlockSpec(state_block, lambda b, d, j: (b, 0, 0, 0, 0))
    seq_in = RET_SEQ_REFS
    if has_h0:
        seq_in = seq_in + (len(in_specs),)
        in_specs.append(state_spec)
        args.append(h0)
        out_specs, out_shape = y_spec, y_shape
    else:
        out_specs = [y_spec, state_spec]
        out_shape = [y_shape, jax.ShapeDtypeStruct((bsz,) + state_block[1:], F32)]
    return pl.pallas_call(
        functools.partial(_per_sequence, _ret_seq, len(in_specs), seq_in, 2, nc, nb, has_h0),
        grid=(bsz // nb, 2, nc),
        in_specs=in_specs,
        out_specs=out_specs,
        out_shape=out_shape,
        scratch_shapes=[pltpu.VMEM((nb, H_RET // 2, 128, 128), F32),
                        pltpu.VMEM((nb, nc, CHUNK, D_RET), BF16)],
        compiler_params=_params(("arbitrary", "arbitrary", "arbitrary")),
        name="ret_latent" if has_h0 else "ret_ctx",
    )(*args)


DA_QUERY_BLOCK = 512
DA_KEY_CHUNK = 512
DA_SUM_ROWS = 16


def _da_kernel(lam_init, seg_lens, *refs):
    n_seg = len(seg_lens)
    qt_ref = refs[0]
    k_refs = refs[1:1 + n_seg]
    vt_refs = refs[1 + n_seg:1 + 2 * n_seg]
    lamp_ref, nw_ref, o_ref, s_scr, p_scr, va_scr = refs[1 + 2 * n_seg:]
    tq = qt_ref.shape[2]
    lk = sum(seg_lens)
    va_rows = DA_HEAD_DIM + DA_SUM_ROWS

    @pl.when(pl.program_id(1) == 0)
    def _():
        for h in range(H_DA):
            col = 0
            for seg in range(n_seg):
                keys = slice(col, col + seg_lens[seg])
                va_scr[h, 0:DA_HEAD_DIM, keys] = vt_refs[seg][0, DA_HEAD_DIM * h:DA_HEAD_DIM * (h + 1), :]
                va_scr[h, DA_HEAD_DIM:va_rows, keys] = jnp.ones((DA_SUM_ROWS, seg_lens[seg]), BF16)
                col += seg_lens[seg]

    lp = lamp_ref[...]
    s1 = jnp.sum(lp[0:1] * lp[1:2], axis=-1, keepdims=True)
    s2 = jnp.sum(lp[2:3] * lp[3:4], axis=-1, keepdims=True)
    lam = jnp.exp(s1) - jnp.exp(s2) + lam_init
    qt = qt_ref[0]
    half_of_row = lax.shift_right_logical(lax.broadcasted_iota(jnp.int32, (D_DA, 1), 0), 5)
    chunks, col = [], 0
    for seg in range(n_seg):
        size = min(DA_KEY_CHUNK, seg_lens[seg])
        for start in range(0, seg_lens[seg], size):
            chunks.append((seg, start, col, size))
            col += size
    n_soft = 2 * H_DA
    q_half, m8 = {}, {}

    def scores(j, ci):
        seg, start, col, size = chunks[ci]
        if j not in q_half:
            q_half[j] = jnp.where(half_of_row == j, qt, jnp.zeros_like(qt))
        k_c = k_refs[seg][0, start:start + size, :].astype(BF16)
        s = _dot(k_c, q_half[j])
        s_scr[j % 2, col:col + size, :] = s
        for r in range(size // 8):
            part = s[8 * r:8 * (r + 1), :]
            m8[j] = part if j not in m8 else jnp.maximum(m8[j], part)

    for ci in range(len(chunks)):
        scores(0, ci)
    o_heads, pv = [], []
    for j in range(n_soft):
        m = jnp.max(m8[j], axis=0, keepdims=True)
        for ci in range(len(chunks)):
            if j + 1 < n_soft:
                scores(j + 1, ci)
            keys = slice(chunks[ci][2], chunks[ci][2] + chunks[ci][3])
            p_scr[j % 2, keys, :] = jnp.exp2(s_scr[j % 2, keys, :] - m).astype(BF16)
        pv.append(_dot(va_scr[j // 2], p_scr[j % 2]))
        if j % 2 == 1:
            inv0 = 1.0 / pv[0][DA_HEAD_DIM:DA_HEAD_DIM + 1, :]
            inv1 = 1.0 / pv[1][DA_HEAD_DIM:DA_HEAD_DIM + 1, :]
            o_h = pv[0][:DA_HEAD_DIM, :] * inv0 - pv[1][:DA_HEAD_DIM, :] * (lam * inv1)
            ms = jnp.mean(o_h * o_h, axis=0, keepdims=True)
            o_heads.append(o_h * lax.rsqrt(ms + EPS) * nw_ref[...] * (1.0 - lam_init))
            pv = []
    o_ref[0] = jnp.concatenate(o_heads, axis=0).T


def _da_call(qt, ks, vts, lam_params, norm_w, lam_init, key_slot=0):
    bsz, _, lq = qt.shape
    seg_lens = tuple(vt.shape[2] for vt in vts)
    lk = sum(seg_lens)
    tq = min(DA_QUERY_BLOCK, lq)
    in_specs = [pl.BlockSpec((1, D_DA, tq), lambda b, i: (b, 0, i))]
    in_specs += [pl.BlockSpec((1, n, D_DA), lambda b, i: (b, key_slot, 0)) for n in seg_lens]
    in_specs += [pl.BlockSpec((1, D_DA, n), lambda b, i: (b, 0, 0)) for n in seg_lens]
    in_specs += [pl.BlockSpec((4, DA_HALF), lambda b, i: (0, 0)),
                 pl.BlockSpec((DA_HEAD_DIM, 1), lambda b, i: (0, 0))]
    return pl.pallas_call(
        functools.partial(_da_kernel, lam_init, seg_lens),
        grid=(bsz, lq // tq),
        in_specs=in_specs,
        out_specs=pl.BlockSpec((1, tq, D_DA), lambda b, i: (b, i, 0)),
        out_shape=jax.ShapeDtypeStruct((bsz, lq, D_DA), F32),
        scratch_shapes=[pltpu.VMEM((2, lk, tq), F32),
                        pltpu.VMEM((2, lk, tq), BF16),
                        pltpu.VMEM((H_DA, DA_HEAD_DIM + DA_SUM_ROWS, lk), BF16)],
        compiler_params=_params(("arbitrary", "arbitrary")),
        name="diff_attn",
    )(qt, *ks, *vts, lam_params, norm_w.reshape(DA_HEAD_DIM, 1))


def _out_proj_kernel(ys_ref, yd_ref, yr_ref, x_ref, mod_ref, w_ref, g_ref, b_ref, rwt_ref, rb_ref,
                     x1_ref, xm2_ref, combt_ref, grp_ref):
    mix = (_dot(ys_ref[...].astype(BF16), w_ref[0:D_SSD, :])
           + _dot(yd_ref[...].astype(BF16), w_ref[D_SSD:D_SSD + D_DA, :])
           + _dot(yr_ref[...].astype(BF16), w_ref[D_SSD + D_DA:, :]))
    mod = mod_ref[0]
    x1 = _layer_norm(ALPHA * x_ref[...] + mod[2:3] * mix, g_ref[...], b_ref[...])
    x1_ref[...] = x1
    xm2 = x1 * (1.0 + mod[4:5]) + mod[3:4]
    xm2_ref[...] = xm2.astype(BF16)

    logits = lax.dot_general(rwt_ref[...], xm2, (((1,), (1,)), ((), ())),
                             preferred_element_type=F32, precision=HIGHEST)
    ex = jnp.exp(logits - jnp.max(logits, axis=0, keepdims=True))
    scores = ex / jnp.sum(ex, axis=0, keepdims=True)
    sel = scores + rb_ref[...]
    row = lax.broadcasted_iota(jnp.int32, sel.shape, 0)
    row_f = row.astype(F32)
    gs = []
    for g in range(N_GROUPS):
        v = [sel[EXPERTS_PER_GROUP * g + i:EXPERTS_PER_GROUP * g + i + 1, :] for i in range(EXPERTS_PER_GROUP)]
        pair_sums = [v[a] + v[b] for a in range(EXPERTS_PER_GROUP) for b in range(a + 1, EXPERTS_PER_GROUP)]
        gs.append(functools.reduce(jnp.maximum, pair_sums))
    best = functools.reduce(jnp.maximum, gs)
    grp = jnp.full(best.shape, N_GROUPS - 1, jnp.int32)
    for g in range(N_GROUPS - 2, -1, -1):
        grp = jnp.where(gs[g] == best, g, grp)
    masked = jnp.where(lax.shift_right_logical(row, 2) == grp, sel, NEG_INF)
    m1 = jnp.max(masked, axis=0, keepdims=True)
    i1 = jnp.min(jnp.where(masked == m1, row_f, float(N_EXPERTS)), axis=0, keepdims=True)
    rest = jnp.where(row_f == i1, NEG_INF, masked)
    m2 = jnp.max(rest, axis=0, keepdims=True)
    i2 = jnp.min(jnp.where(rest == m2, row_f, float(N_EXPERTS)), axis=0, keepdims=True)
    picked = jnp.where((row_f == i1) | (row_f == i2), scores, 0.0)
    combt_ref[...] = picked / jnp.sum(picked, axis=0, keepdims=True)
    grp_ref[...] = grp


def _out_proj_call(y_ssd, y_da, y_ret, x2d, mod, w_out, ln_g, ln_b, router_w, router_b, seq_len):
    n = x2d.shape[0]
    tm = 512
    per_seq = max(seq_len // tm, 1)
    mod_idx = (lambda i: (i // per_seq, 0, 0)) if mod.shape[0] > 1 else (lambda i: (0, 0, 0))
    row = lambda w: pl.BlockSpec((tm, w), lambda i: (i, 0))
    col = lambda h: pl.BlockSpec((h, tm), lambda i: (0, i))
    full = lambda s: pl.BlockSpec(s, lambda i: (0,) * len(s))
    return pl.pallas_call(
        _out_proj_kernel,
        grid=(n // tm,),
        in_specs=[row(D_SSD), row(D_DA), row(D_RET), row(D_MODEL),
                  pl.BlockSpec((1, 8, D_MODEL), mod_idx),
                  full((D_MODEL, D_MODEL)), full((1, D_MODEL)), full((1, D_MODEL)),
                  full((N_EXPERTS, D_MODEL)), full((N_EXPERTS, 1))],
        out_specs=[row(D_MODEL), row(D_MODEL), col(N_EXPERTS), col(1)],
        out_shape=[jax.ShapeDtypeStruct((n, D_MODEL), F32),
                   jax.ShapeDtypeStruct((n, D_MODEL), BF16),
                   jax.ShapeDtypeStruct((N_EXPERTS, n), F32),
                   jax.ShapeDtypeStruct((1, n), jnp.int32)],
        compiler_params=_params(("arbitrary",)),
        name="out_proj_router",
    )(y_ssd, y_da, y_ret, x2d, mod, w_out, ln_g.reshape(1, D_MODEL), ln_b.reshape(1, D_MODEL),
      router_w.T, router_b.reshape(N_EXPERTS, 1))


MOE_T = 1024
MOE_TILE_LOG2 = 6
MOE_TILE = 1 << MOE_TILE_LOG2
MOE_SLOTS = MOE_T + N_GROUPS * MOE_TILE
MOE_COL = 256
MOE_FILL_ROWS = 128
MOE_RUN_TILES = 4
MOE_EXPERTS_PER_STEP = 2


def _moe_kernel(xm_ref, combt_ref, grp_ref, wg_ref, wu_ref, wd_ref, x1_ref, mod_ref, g_ref, b_ref, o_ref,
                p_ref, xs_ref, cs_ref, y_ref, seg_ref):
    step = pl.program_id(1)
    n_tiles = MOE_SLOTS // MOE_TILE

    @pl.when(step == 0)
    def _():
        grp = grp_ref[...]
        onehot = lax.broadcasted_iota(jnp.int32, (N_GROUPS, MOE_T), 0) == grp
        onehot_f = jnp.where(onehot, 1.0, 0.0)
        bi = lax.broadcasted_iota(jnp.int32, (128, 128), 0)
        bj = lax.broadcasted_iota(jnp.int32, (128, 128), 1)
        strict_upper = jnp.where(bi < bj, 1.0, 0.0).astype(BF16)
        running = jnp.zeros((N_GROUPS, 1), F32)
        ranks = []
        for blk in range(MOE_T // 128):
            oh_b = onehot_f[:, 128 * blk:128 * (blk + 1)]
            ranks.append(_dot(oh_b.astype(BF16), strict_upper) + running)
            running = running + jnp.sum(oh_b, axis=1, keepdims=True)
        rank = jnp.concatenate(ranks, axis=1)
        tiles = lax.shift_right_logical(running.astype(jnp.int32) + (MOE_TILE - 1), MOE_TILE_LOG2)
        starts = [jnp.zeros((1, 1), jnp.int32)]
        for g in range(1, N_GROUPS):
            starts.append(starts[-1] + tiles[g - 1:g, :])
        used = starts[-1] + tiles[N_GROUPS - 1:, :]
        start = jnp.concatenate(starts, axis=0)
        pos = jnp.sum(jnp.where(onehot, (start * MOE_TILE).astype(F32) + rank, 0.0), axis=0, keepdims=True)
        pos = pos.astype(jnp.int32)
        for g in range(N_GROUPS):
            seg_ref[g] = jnp.sum(start[g:g + 1, :])
            seg_ref[N_GROUPS + g] = jnp.sum(tiles[g:g + 1, :])

        def fill(i, c):
            off = pl.multiple_of(i * MOE_FILL_ROWS, MOE_FILL_ROWS)
            slot = lax.broadcasted_iota(jnp.int32, (MOE_FILL_ROWS, MOE_T), 0) + off
            p_ref[pl.ds(off, MOE_FILL_ROWS), :] = jnp.where(slot == pos, 1.0, 0.0).astype(BF16)
            return c
        lax.fori_loop(0, MOE_SLOTS // MOE_FILL_ROWS, fill, 0)

        p = p_ref[...]
        for cb in range(D_MODEL // MOE_COL):
            cols = slice(MOE_COL * cb, MOE_COL * (cb + 1))
            xs_ref[:, cols] = _dot(p, xm_ref[:, cols]).astype(BF16)
        ct = combt_ref[...]
        hi = ct.astype(BF16)
        r1 = ct - hi.astype(F32)
        mid = r1.astype(BF16)
        lo = (r1 - mid.astype(F32)).astype(BF16)
        pieces = jnp.concatenate([hi, mid, lo], axis=0)
        perm = lax.dot_general(pieces, p, (((1,), (1,)), ((), ())), preferred_element_type=F32)
        cs_ref[...] = (perm[:N_EXPERTS] + perm[N_EXPERTS:2 * N_EXPERTS] + perm[2 * N_EXPERTS:]).T

        def clear(i, c):
            off = pl.multiple_of(i * MOE_TILE, MOE_TILE)
            y_ref[pl.ds(off, MOE_TILE), :] = jnp.zeros((MOE_TILE, D_MODEL), F32)
            return c
        lax.fori_loop(jnp.sum(used), n_tiles, clear, 0)

    lane = lax.broadcasted_iota(jnp.int32, (1, N_EXPERTS), 1)

    def run_expert(e, k):
        g_e = lax.shift_right_logical(e, 2)
        first_tile = seg_ref[g_e]
        group_tiles = seg_ref[N_GROUPS + g_e]
        first_in_group = (e & (EXPERTS_PER_GROUP - 1)) == 0

        def expert_rows(tile, rows):
            off = pl.multiple_of(tile * MOE_TILE, MOE_TILE)
            xs = xs_ref[pl.ds(off, rows), :]
            h = _silu(_dot(xs, wg_ref[0, k])) * _dot(xs, wu_ref[0, k])
            y = _dot(h.astype(BF16), wd_ref[0, k])
            w_e = jnp.sum(jnp.where(lane == e, cs_ref[pl.ds(off, rows), :], 0.0), axis=-1, keepdims=True)

            @pl.when(first_in_group)
            def _():
                y_ref[pl.ds(off, rows), :] = w_e * y

            @pl.when(jnp.logical_not(first_in_group))
            def _():
                y_ref[pl.ds(off, rows), :] += w_e * y

        def full_run(i, c):
            expert_rows(first_tile + MOE_RUN_TILES * i, MOE_RUN_TILES * MOE_TILE)
            return c
        n_full = lax.shift_right_logical(group_tiles, 2)
        lax.fori_loop(0, n_full, full_run, 0)
        rest_tile = first_tile + MOE_RUN_TILES * n_full
        for run in (2, 1):
            @pl.when((group_tiles & run) != 0)
            def _(run=run):
                done = group_tiles & (MOE_RUN_TILES - 1) & ~(2 * run - 1)
                expert_rows(rest_tile + done, run * MOE_TILE)

    for k in range(MOE_EXPERTS_PER_STEP):
        run_expert(step * MOE_EXPERTS_PER_STEP + k, k)

    @pl.when(step == N_EXPERTS // MOE_EXPERTS_PER_STEP - 1)
    def _():
        p = p_ref[...]
        for cb in range(D_MODEL // MOE_COL):
            cols = slice(MOE_COL * cb, MOE_COL * (cb + 1))
            o_ref[:, cols] = lax.dot_general(p, y_ref[:, cols].astype(BF16), (((0,), (0,)), ((), ())),
                                             preferred_element_type=F32)
        mod = mod_ref[0]
        o_ref[...] = _layer_norm(ALPHA * x1_ref[...] + mod[5:6] * o_ref[...], g_ref[...], b_ref[...])


def _moe_call(xm2, combt, grp, wg, wu, wd, layer, x1, mod, ln_g, ln_b, seq_len):
    n = xm2.shape[0]
    per_seq = max(seq_len // MOE_T, 1)
    mod_idx = (lambda i, e: (i // per_seq, 0, 0)) if mod.shape[0] > 1 else (lambda i, e: (0, 0, 0))
    row = lambda w: pl.BlockSpec((MOE_T, w), lambda i, e: (i, 0))
    col = lambda h: pl.BlockSpec((h, MOE_T), lambda i, e: (0, i))
    return pl.pallas_call(
        _moe_kernel,
        grid=(n // MOE_T, N_EXPERTS // MOE_EXPERTS_PER_STEP),
        in_specs=[row(D_MODEL), col(N_EXPERTS), col(1),
                  pl.BlockSpec((1, MOE_EXPERTS_PER_STEP, D_MODEL, D_FF), lambda i, e: (layer, e, 0, 0)),
                  pl.BlockSpec((1, MOE_EXPERTS_PER_STEP, D_MODEL, D_FF), lambda i, e: (layer, e, 0, 0)),
                  pl.BlockSpec((1, MOE_EXPERTS_PER_STEP, D_FF, D_MODEL), lambda i, e: (layer, e, 0, 0)),
                  row(D_MODEL),
                  pl.BlockSpec((1, 8, D_MODEL), mod_idx),
                  pl.BlockSpec((1, D_MODEL), lambda i, e: (0, 0)),
                  pl.BlockSpec((1, D_MODEL), lambda i, e: (0, 0))],
        out_specs=row(D_MODEL),
        out_shape=jax.ShapeDtypeStruct((n, D_MODEL), F32),
        scratch_shapes=[pltpu.VMEM((MOE_SLOTS, MOE_T), BF16),
                        pltpu.VMEM((MOE_SLOTS, D_MODEL), BF16),
                        pltpu.VMEM((MOE_SLOTS, N_EXPERTS), F32),
                        pltpu.VMEM((MOE_SLOTS, D_MODEL), F32),
                        pltpu.SMEM((2 * N_GROUPS,), jnp.int32)],
        compiler_params=_params(("arbitrary", "arbitrary")),
        name="moe_grouped",
    )(xm2, combt, grp, wg, wu, wd, x1, mod, ln_g.reshape(1, D_MODEL), ln_b.reshape(1, D_MODEL))


def _pack_in_proj(w_in_l, rope):
    z, xbc, dtw, da, ret = jnp.split(w_in_l, (512, 1280, 1296, 2064), axis=1)
    wq, wk, wv = jnp.split(da, 3, axis=1)
    wrq, wrk, wrv, wrg = jnp.split(ret, 4, axis=1)
    nn_w = {"z": z, "xbc": xbc, "k": wk, "rq": wrq, "rv": wrv, "rg": wrg, "v": wv,
            "dt": jnp.pad(dtw, ((0, 0), (0, DT_COLS - dtw.shape[1])))}
    nt_w = {"q": wq.T, "v": wv.T, "rk": wrk.T, "dt": jnp.pad(dtw.T, ((0, DT_ROWS - dtw.shape[1]), (0, 0)))}
    cols, rows = _in_proj_layout(rope)
    return (jnp.concatenate([nn_w[name] for name in cols], axis=1).astype(BF16),
            jnp.concatenate([nt_w[name] for name in rows], axis=0).astype(BF16))


def _rope_tables(seq_len, dim, reps):
    nf = dim // 4
    inv = ROPE_BASE ** (-jnp.arange(nf, dtype=F32) / nf)
    t = jnp.arange(seq_len)
    r = (t // GRID_W).astype(F32)[:, None] * inv
    c = (t % GRID_W).astype(F32)[:, None] * inv
    ang = jnp.tile(jnp.concatenate([r, r, c, c], axis=-1), (1, reps))
    return jnp.cos(ang), jnp.sin(ang)


def _layer(x, mod, l, P, ctx, rope_tabs, kv_stacks=None):
    bsz, seq_len, _ = x.shape
    n = bsz * seq_len
    x2d = x.reshape(n, D_MODEL)
    latent = ctx is not None
    wnn, wnt = P["in_proj"][l][1 if latent else 0]
    outs = _in_proj_call(x2d, mod, wnn, wnt, seq_len, rope_tabs, l, kv_stacks)
    z, xbc, dt, dtt, qt, k, vt, rq, rkt, rv, rg = outs[:11]
    r3 = lambda a: a.reshape(bsz, -1, a.shape[-1])

    h_ssd0 = ctx[2] if latent else None
    h_ret0 = ctx[3] if latent else None
    ssd_out = _ssd_call(r3(xbc), r3(z), r3(dt), dtt, P["ssd_conv_w"][l], P["ssd_conv_b"][l],
                        P["ssd_dt_bias"][l], P["ssd_a_log"][l], P["ssd_d"][l], P["ssd_norm_w"][l], h_ssd0)
    ret_out = _ret_call(r3(rq), rkt, r3(rv), r3(rg), P["ret_decay"][l], P["ret_norm_w"][l], h_ret0)

    ks, vts = [r3(k)], [vt]
    if latent:
        y_ssd, y_ret = ssd_out, ret_out
        new = None
        ks.append(ctx[0].reshape(bsz, -1, D_DA).astype(BF16))
        vts.append(jnp.transpose(ctx[1].reshape(bsz, -1, D_DA), (0, 2, 1)).astype(BF16))
    else:
        y_ssd, hs = ssd_out
        y_ret, hr = ret_out
        new = ((k, outs[11]), hs, hr)
    lam_init = 0.8 - 0.6 * math.exp(-0.3 * l)
    y_da = _da_call(qt, ks, vts, P["da_lambda"][l], P["da_norm_w"][l], lam_init, 0 if latent else l)

    x1, xm2, combt, grp = _out_proj_call(y_ssd.reshape(n, D_SSD), y_da.reshape(n, D_DA),
                                         y_ret.reshape(n, D_RET), x2d, mod, P["w_out16"][l],
                                         P["ln_mix_g"][l], P["ln_mix_b"][l], P["router_w"], P["router_b"],
                                         seq_len)
    x2 = _moe_call(xm2, combt, grp, P["wg16"], P["wu16"], P["wd16"], l, x1, mod,
                   P["ln_ffn_g"][l], P["ln_ffn_b"][l], seq_len)
    return x2.reshape(bsz, seq_len, D_MODEL), new


def kernel(x_prompt, x_sample, c, cache_da_k, cache_da_v, state_ssd, state_ret, c_ctx, w_ada, b_ada, w_in,
           ssd_conv_w, ssd_conv_b, ssd_dt_bias, ssd_a_log, ssd_d, ssd_norm_w, da_lambda, da_norm_w, ret_decay,
           ret_norm_w, w_out, ln_mix_g, ln_mix_b, router_w, router_b, moe_w_gate, moe_w_up, moe_w_down,
           ln_ffn_g, ln_ffn_b):
    dec_b = x_sample.shape[0]
    P = dict(ssd_conv_w=ssd_conv_w, ssd_conv_b=ssd_conv_b, ssd_dt_bias=ssd_dt_bias, ssd_a_log=ssd_a_log,
             ssd_d=ssd_d, ssd_norm_w=ssd_norm_w, da_lambda=da_lambda, da_norm_w=da_norm_w,
             ret_decay=ret_decay, ret_norm_w=ret_norm_w, ln_mix_g=ln_mix_g, ln_mix_b=ln_mix_b,
             router_w=router_w, router_b=router_b, ln_ffn_g=ln_ffn_g, ln_ffn_b=ln_ffn_b)
    P["in_proj"] = [(_pack_in_proj(w_in[l], False), _pack_in_proj(w_in[l], True)) for l in range(DEPTH)]
    P["w_out16"] = w_out.astype(BF16)
    P["wg16"] = moe_w_gate.astype(BF16)
    P["wu16"] = moe_w_up.astype(BF16)
    P["wd16"] = moe_w_down.astype(BF16)

    cond8 = jnp.concatenate([c_ctx[None, :], c, jnp.zeros((8 - 1 - dec_b, D_MODEL), F32)], axis=0)
    mod_all = _ada_call(cond8, w_ada, b_ada).reshape(DEPTH, 8, 6, D_MODEL)
    mod_all = jnp.pad(mod_all, ((0, 0), (0, 0), (0, 2), (0, 0)))

    y = x_prompt
    kv_stacks, hs_, hr_ = None, [], []
    for l in range(DEPTH):
        y, (kv_stacks, hs_l, hr_l) = _layer(y, mod_all[l, 0:1], l, P, None, None, kv_stacks)
        hs_.append(hs_l)
        hr_.append(hr_l)

    def stack_layers(parts):
        bsz = parts[0].shape[0]
        flat = jnp.concatenate([p.reshape(bsz, 1, -1) for p in parts], axis=1)
        return flat.reshape((bsz, len(parts)) + parts[0].shape[1:])

    kv_shape = (x_prompt.shape[0], DEPTH, x_prompt.shape[1], H_DA, DA_HEAD_DIM)
    new_da_k = kv_stacks[0].reshape(kv_shape)
    new_da_v = kv_stacks[1].reshape(kv_shape)
    new_ssd = stack_layers(hs_)
    new_ret = stack_layers(hr_)

    seq_len = x_sample.shape[1]
    cos_da, sin_da = _rope_tables(seq_len, DA_HALF, D_DA // DA_HALF)
    cos_ret, sin_ret = _rope_tables(seq_len, RET_HEAD_DIM, H_RET)
    rope_tabs = (cos_da, sin_da, cos_ret, sin_ret, cos_da.T, sin_da.T, cos_ret.T, sin_ret.T)
    zl = x_sample
    for l in range(DEPTH):
        ctx = (cache_da_k[:, l], cache_da_v[:, l], state_ssd[:, l], state_ret[:, l])
        zl, _ = _layer(zl, mod_all[l, 1:1 + dec_b], l, P, ctx, rope_tabs)

    return (y, zl, new_da_k, new_da_v, new_ssd, new_ret)
```
